```python
import numpy as np
import jax, jax.numpy as jnp
from jax import lax


D_MODEL = 2048
BATCH = 2
SEQ = 16384
DEPTH = 1

MIX_WIDTH = D_MODEL
CONV_WIDTH = MIX_WIDTH // 2
CONV_GROUP = 128
CONV_KSIZE = 3
HEAD_DIM = 128
N_HEADS = (MIX_WIDTH - CONV_WIDTH) // HEAD_DIM
N_KV = 2
GROUP_R = N_HEADS // N_KV
ATTN_WIDTH = N_HEADS * HEAD_DIM
KV_WIDTH = N_KV * HEAD_DIM
CMP_BLOCK = 32
CMP_STRIDE = 16
CMP_HIDDEN = 2 * HEAD_DIM
SEL_BLOCK = 64
SEL_TOP_N = 16
WINDOW = 512
Q_BLOCK = 128
ROPE_THETA = 10000.0
D_FF = -(-8 * D_MODEL // (3 * 256)) * 256
IN_WIDTH = 3 * CONV_WIDTH + ATTN_WIDTH + 6 * KV_WIDTH + 3 * N_HEADS
EPS = 1e-6
NEG_INF = -1e30
FORCE_BONUS = 1e4

kernel_name = 'hymba_conv_nsa_adaln_block'


def rms_norm(x, g):
    xf = x.astype(jnp.float32)
    y = xf * lax.rsqrt(jnp.mean(xf * xf, axis=-1, keepdims=True) + EPS)
    return (y * g.astype(jnp.float32)).astype(x.dtype)


def modulate(h, shift, scale):
    return h * (1.0 + scale[:, None, :]) + shift[:, None, :]


def rope_tables(pos):
    inv = 1.0 / (ROPE_THETA ** (jnp.arange(0, HEAD_DIM, 2, dtype=jnp.float32) / HEAD_DIM))
    ang = pos.astype(jnp.float32)[..., None] * inv
    return jnp.cos(ang), jnp.sin(ang)


def apply_rope(x, cos, sin):
    cos = cos[:, :, None, :].astype(x.dtype)
    sin = sin[:, :, None, :].astype(x.dtype)
    x1, x2 = jnp.split(x, 2, axis=-1)
    return jnp.concatenate([x1 * cos - x2 * sin, x2 * cos + x1 * sin], axis=-1)


def masked_softmax(s, mask):
    return jax.nn.softmax(jnp.where(mask, s.astype(jnp.float32), NEG_INF), axis=-1)


def short_conv_mixer(b_gate, c_gate, h, conv_w):
    u = c_gate * h
    y = lax.conv_general_dilated(u, conv_w.astype(u.dtype)[:, None, :], window_strides=(1,),
                                 padding=[(CONV_KSIZE - 1, 0)],
                                 dimension_numbers=('NWC', 'WIO', 'NWC'),
                                 feature_group_count=CONV_WIDTH)
    return b_gate * y


def compress(kv, pe, w1, b1, w2):
    S = kv.shape[1]
    n_cmp = (S - CMP_BLOCK) // CMP_STRIDE + 1
    idx = np.arange(n_cmp)[:, None] * CMP_STRIDE + np.arange(CMP_BLOCK)[None, :]
    blocks = kv[:, idx] + pe[None, None, :, None, :]
    hid = jax.nn.gelu(jnp.einsum('bnlgd,ldh->bngh', blocks, w1) + b1)
    return jnp.einsum('bngh,he->bnge', hid, w2)


def nsa_mixer(q, k_cmp_raw, v_cmp_raw, k_slc, v_slc, k_win, v_win, gate_logits, positions,
              q_norm, k_norm, pe_k, k_w1, k_b1, k_w2, pe_v, v_w1, v_b1, v_w2):
    B, S = q.shape[0], q.shape[1]
    cos, sin = rope_tables(positions)
    q = apply_rope(rms_norm(q, q_norm), cos, sin)
    k_slc = apply_rope(rms_norm(k_slc, k_norm[1]), cos, sin)
    k_win = apply_rope(rms_norm(k_win, k_norm[2]), cos, sin)

    k_cmp = compress(k_cmp_raw, pe_k, k_w1, k_b1, k_w2)
    v_cmp = compress(v_cmp_raw, pe_v, v_w1, v_b1, v_w2)
    n_cmp = k_cmp.shape[1]
    cmp_start = np.arange(n_cmp) * CMP_STRIDE
    cmp_end = cmp_start + CMP_BLOCK - 1
    k_cmp = apply_rope(rms_norm(k_cmp, k_norm[0]), cos[:, cmp_end], sin[:, cmp_end])

    n_sel = S // SEL_BLOCK
    n_top = min(SEL_TOP_N, n_sel)
    sel_start = np.arange(n_sel) * SEL_BLOCK
    overlap = jnp.asarray(((cmp_start[:, None] < sel_start[None, :] + SEL_BLOCK)
                           & (cmp_end[:, None] >= sel_start[None, :])).astype(np.float32))
    cmp_end_j = jnp.asarray(cmp_end)

    ks_blocks = k_slc.reshape(B, n_sel, SEL_BLOCK, N_KV, HEAD_DIM).transpose(0, 3, 1, 2, 4)
    vs_blocks = v_slc.reshape(B, n_sel, SEL_BLOCK, N_KV, HEAD_DIM).transpose(0, 3, 1, 2, 4)
    pad = ((0, 0), (WINDOW, 0), (0, 0), (0, 0))
    kw_pad = jnp.pad(k_win, pad)
    vw_pad = jnp.pad(v_win, pad)
    qg = q.reshape(B, S, N_KV, GROUP_R, HEAD_DIM)
    gates = jax.nn.sigmoid(gate_logits.astype(jnp.float32)).astype(q.dtype)
    scale = HEAD_DIM ** -0.5
    gather_blocks = jax.vmap(jax.vmap(lambda blk, ix: blk[ix]))

    def block_fn(qi):
        q0 = qi * Q_BLOCK
        t = q0 + jnp.arange(Q_BLOCK)
        qb = lax.dynamic_slice_in_dim(qg, q0, Q_BLOCK, axis=1) * scale

        m_c = cmp_end_j[None, :] <= t[:, None]
        p_c = masked_softmax(jnp.einsum('bqgrd,bngd->bgrqn', qb, k_cmp), m_c)
        p_c = jnp.where(m_c, p_c, 0.0)
        o_cmp = jnp.einsum('bgrqn,bngd->bqgrd', p_c.astype(v_cmp.dtype), v_cmp)

        imp = jnp.einsum('bgrqn,nj->bgqj', p_c, overlap)
        j = jnp.arange(n_sel)[None, :]
        cur = (t // SEL_BLOCK)[:, None]
        valid = j * SEL_BLOCK <= t[:, None]
        forced = (j == 0) | (j == cur) | (j == cur - 1)
        score = jnp.where(valid, imp + FORCE_BONUS * forced, NEG_INF)
        _, idx = lax.top_k(score, n_top)
        ks = gather_blocks(ks_blocks, idx)
        vs = gather_blocks(vs_blocks, idx)
        tok = idx[..., None] * SEL_BLOCK + jnp.arange(SEL_BLOCK)
        m_s = (tok <= t[None, None, :, None, None]).reshape(B, N_KV, 1, Q_BLOCK, n_top * SEL_BLOCK)
        s_s = jnp.einsum('bqgrd,bgqnld->bgrqnl', qb, ks).reshape(B, N_KV, GROUP_R, Q_BLOCK, n_top * SEL_BLOCK)
        p_s = masked_softmax(s_s, m_s)
        o_slc = jnp.einsum('bgrqk,bgqkd->bqgrd', p_s.astype(vs.dtype),
                           vs.reshape(B, N_KV, Q_BLOCK, n_top * SEL_BLOCK, HEAD_DIM))

        kw = lax.dynamic_slice_in_dim(kw_pad, q0, WINDOW + Q_BLOCK, axis=1)
        vw = lax.dynamic_slice_in_dim(vw_pad, q0, WINDOW + Q_BLOCK, axis=1)
        kp = q0 - WINDOW + jnp.arange(WINDOW + Q_BLOCK)
        m_w = (kp[None, :] <= t[:, None]) & (kp[None, :] > t[:, None] - WINDOW) & (kp[None, :] >= 0)
        p_w = masked_softmax(jnp.einsum('bqgrd,bkgd->bgrqk', qb, kw), m_w)
        o_win = jnp.einsum('bgrqk,bkgd->bqgrd', p_w.astype(vw.dtype), vw)

        g = lax.dynamic_slice_in_dim(gates, q0, Q_BLOCK, axis=1).reshape(B, Q_BLOCK, N_KV, GROUP_R, 3)
        o = g[..., 0:1] * o_cmp + g[..., 1:2] * o_slc + g[..., 2:3] * o_win
        return o.reshape(B, Q_BLOCK, ATTN_WIDTH)

    out = lax.map(block_fn, jnp.arange(S // Q_BLOCK))
    return out.transpose(1, 0, 2, 3).reshape(B, S, ATTN_WIDTH)


def setup_inputs(seed: int = 0) -> dict:
    key = jax.random.key(seed)
    ks = jax.random.split(key, 24)
    f32 = jnp.float32
    L = DEPTH

    def nrm(k, shape, s):
        return jax.random.normal(k, shape, f32) * s

    def gain(k, shape):
        return 1.0 + 0.02 * jax.random.normal(k, shape, f32)

    return {
        'x': nrm(ks[0], (BATCH, SEQ, D_MODEL), 1.0),
        'c': nrm(ks[1], (BATCH, D_MODEL), 1.0),
        'positions': jnp.broadcast_to(jnp.arange(SEQ, dtype=jnp.int32), (BATCH, SEQ)),
        'ada_w': nrm(ks[2], (L, D_MODEL, 6 * D_MODEL), 0.5 * D_MODEL ** -0.5),
        'ada_b': nrm(ks[3], (L, 6 * D_MODEL), 0.01),
        'norm_mix': gain(ks[4], (L, D_MODEL)),
        'norm_ffn': gain(ks[5], (L, D_MODEL)),
        'w_in': nrm(ks[6], (L, D_MODEL, IN_WIDTH), D_MODEL ** -0.5),
        'conv_w': nrm(ks[7], (L, CONV_KSIZE, CONV_WIDTH), CONV_KSIZE ** -0.5),
        'cmp_pe_k': nrm(ks[8], (L, CMP_BLOCK, HEAD_DIM), 0.1),
        'cmp_k_w1': nrm(ks[9], (L, CMP_BLOCK, HEAD_DIM, CMP_HIDDEN), (CMP_BLOCK * HEAD_DIM) ** -0.5),
        'cmp_k_b1': nrm(ks[10], (L, CMP_HIDDEN), 0.01),
        'cmp_k_w2': nrm(ks[11], (L, CMP_HIDDEN, HEAD_DIM), CMP_HIDDEN ** -0.5),
        'cmp_pe_v': nrm(ks[12], (L, CMP_BLOCK, HEAD_DIM), 0.1),
        'cmp_v_w1': nrm(ks[13], (L, CMP_BLOCK, HEAD_DIM, CMP_HIDDEN), (CMP_BLOCK * HEAD_DIM) ** -0.5),
        'cmp_v_b1': nrm(ks[14], (L, CMP_HIDDEN), 0.01),
        'cmp_v_w2': nrm(ks[15], (L, CMP_HIDDEN, HEAD_DIM), CMP_HIDDEN ** -0.5),
        'q_norm': gain(ks[16], (L, HEAD_DIM)),
        'k_norm': gain(ks[17], (L, 3, HEAD_DIM)),
        'out_norm_conv': gain(ks[18], (L, CONV_WIDTH)),
        'out_norm_attn': gain(ks[19], (L, ATTN_WIDTH)),
        'w_out': nrm(ks[20], (L, MIX_WIDTH, D_MODEL), MIX_WIDTH ** -0.5),
        'ffn_w1': nrm(ks[21], (L, D_MODEL, D_FF), D_MODEL ** -0.5),
        'ffn_w3': nrm(ks[22], (L, D_MODEL, D_FF), D_MODEL ** -0.5),
        'ffn_w2': nrm(ks[23], (L, D_FF, D_MODEL), D_FF ** -0.5),
    }


def reference(x, c, positions, ada_w, ada_b, norm_mix, norm_ffn, w_in, conv_w,
              cmp_pe_k, cmp_k_w1, cmp_k_b1, cmp_k_w2, cmp_pe_v, cmp_v_w1, cmp_v_b1, cmp_v_w2,
              q_norm, k_norm, out_norm_conv, out_norm_attn, w_out, ffn_w1, ffn_w3, ffn_w2):
    B, S = x.shape[0], x.shape[1]
    splits = [int(v) for v in np.cumsum([CONV_WIDTH] * 3 + [ATTN_WIDTH] + [KV_WIDTH] * 6)]
    for l in range(DEPTH):
        sh_a, sc_a, g_a, sh_f, sc_f, g_f = jnp.split(jax.nn.silu(c) @ ada_w[l] + ada_b[l], 6, axis=-1)

        h = modulate(rms_norm(x, norm_mix[l]), sh_a, sc_a)
        proj = h @ w_in[l]
        cb, cc, ch, q, kc, vc, ksl, vsl, kw, vw, gl = jnp.split(proj, splits, axis=-1)

        y_conv = short_conv_mixer(cb, cc, ch, conv_w[l])
        y_conv = rms_norm(y_conv.reshape(B, S, CONV_WIDTH // CONV_GROUP, CONV_GROUP),
                          out_norm_conv[l].reshape(CONV_WIDTH // CONV_GROUP, CONV_GROUP)).reshape(B, S, CONV_WIDTH)

        y_attn = nsa_mixer(q.reshape(B, S, N_HEADS, HEAD_DIM),
                           kc.reshape(B, S, N_KV, HEAD_DIM), vc.reshape(B, S, N_KV, HEAD_DIM),
                           ksl.reshape(B, S, N_KV, HEAD_DIM), vsl.reshape(B, S, N_KV, HEAD_DIM),
                           kw.reshape(B, S, N_KV, HEAD_DIM), vw.reshape(B, S, N_KV, HEAD_DIM),
                           gl.reshape(B, S, N_HEADS, 3), positions, q_norm[l], k_norm[l],
                           cmp_pe_k[l], cmp_k_w1[l], cmp_k_b1[l], cmp_k_w2[l],
                           cmp_pe_v[l], cmp_v_w1[l], cmp_v_b1[l], cmp_v_w2[l])
        y_attn = rms_norm(y_attn.reshape(B, S, N_HEADS, HEAD_DIM),
                          out_norm_attn[l].reshape(N_HEADS, HEAD_DIM)).reshape(B, S, ATTN_WIDTH)

        x = x + g_a[:, None, :] * (jnp.concatenate([y_conv, y_attn], axis=-1) @ w_out[l])

        h = modulate(rms_norm(x, norm_ffn[l]), sh_f, sc_f)
        x = x + g_f[:, None, :] * ((jax.nn.silu(h @ ffn_w1[l]) * (h @ ffn_w3[l])) @ ffn_w2[l])
    return x
```

```python
import functools

import numpy as np
import jax
import jax.numpy as jnp
from jax import lax
from jax.experimental import pallas as pl
from jax.experimental.pallas import tpu as pltpu

F32 = jnp.float32
BF16 = jnp.bfloat16

CONV_GROUP = 128
CONV_KSIZE = 3
HEAD_DIM = 128
N_KV = 2
GROUP_R = 4
N_HEADS = N_KV * GROUP_R
CMP_BLOCK = 32
CMP_STRIDE = 16
SEL_BLOCK = 64
SEL_TOP_N = 16
WINDOW = 512
ROPE_THETA = 10000.0
EPS = 1e-6
NEG_INF = -1e30

LANES = 128
VMEM_LIMIT = 56 * 1024 * 1024

ROW_TILE = 512
Q_TILE = 128
SEL_CHUNK = 512
CMP_TILE = 256
ADA_TN = 1024
INPROJ_TN = 768
FFN_TF = 512


def _cparams(sem):
    return pltpu.CompilerParams(dimension_semantics=sem, vmem_limit_bytes=VMEM_LIMIT)


def _split(a):
    hi = a.astype(BF16)
    lo = (a - hi.astype(F32)).astype(BF16)
    return hi, lo


def _dot(a, b):
    return jnp.dot(a, b, preferred_element_type=F32)


def _dot_nt(a, b):
    return lax.dot_general(a, b, (((1,), (1,)), ((), ())), preferred_element_type=F32)


def _dot3(a_hi, a_lo, b_hi, b_lo, dot=_dot):
    return (dot(a_hi, b_lo) + dot(a_lo, b_hi)) + dot(a_hi, b_hi)


def _rms(x, gain):
    return x * lax.rsqrt(jnp.mean(x * x, axis=-1, keepdims=True) + EPS) * gain


def _rope(y, cosf, sinf):
    return y * cosf + pltpu.roll(y, HEAD_DIM // 2, 1) * sinf


def _ada_kernel(c_ref, w_ref, b_ref, o_ref):
    s_hi, s_lo = _split(jax.nn.silu(c_ref[...]))
    w_hi, w_lo = _split(w_ref[...])
    o_ref[...] = _dot3(s_hi, s_lo, w_hi, w_lo) + b_ref[...]


def _ada(c8, w, b):
    d, n = w.shape
    return pl.pallas_call(
        _ada_kernel,
        grid=(n // ADA_TN,),
        in_specs=[pl.BlockSpec((8, d), lambda j: (0, 0)),
                  pl.BlockSpec((d, ADA_TN), lambda j: (0, j)),
                  pl.BlockSpec((1, ADA_TN), lambda j: (0, j))],
        out_specs=pl.BlockSpec((8, ADA_TN), lambda j: (0, j)),
        out_shape=jax.ShapeDtypeStruct((8, n), F32),
        compiler_params=_cparams(("arbitrary",)),
        name="ada",
    )(c8, w, b)


def _inproj_kernel(x_ref, sh_ref, sc_ref, g_ref, wm_ref, whi_ref, wlo_ref,
                   om_ref, op_ref, hhi_ref, hlo_ref, *, n_main):
    j = pl.program_id(1)

    @pl.when(j == 0)
    def _():
        h = _rms(x_ref[...], g_ref[...]) * (1.0 + sc_ref[0]) + sh_ref[0]
        hi, lo = _split(h)
        hhi_ref[...] = hi
        hlo_ref[...] = lo

    @pl.when(j < n_main)
    def _():
        om_ref[...] = _dot(hhi_ref[...], wm_ref[...]).astype(om_ref.dtype)

    @pl.when(j == n_main)
    def _():
        op_ref[...] = _dot3(hhi_ref[...], hlo_ref[...], whi_ref[...], wlo_ref[...])


def _inproj(x2, mod3, gain, w_main, w_hi, w_lo, seq):
    t, d = x2.shape
    n_main = w_main.shape[1] // INPROJ_TN
    n_prec = w_hi.shape[1]
    tiles_per_seq = seq // ROW_TILE
    return pl.pallas_call(
        functools.partial(_inproj_kernel, n_main=n_main),
        grid=(t // ROW_TILE, n_main + 1),
        in_specs=[
            pl.BlockSpec((ROW_TILE, d), lambda i, j: (i, 0)),
            pl.BlockSpec((1, 1, d), lambda i, j: (i // tiles_per_seq, 0, 0)),
            pl.BlockSpec((1, 1, d), lambda i, j: (i // tiles_per_seq, 0, 1)),
            pl.BlockSpec((1, d), lambda i, j: (0, 0)),
            pl.BlockSpec((d, INPROJ_TN), lambda i, j: (0, jnp.minimum(j, n_main - 1))),
            pl.BlockSpec((d, n_prec), lambda i, j: (0, 0)),
            pl.BlockSpec((d, n_prec), lambda i, j: (0, 0)),
        ],
        out_specs=[
            pl.BlockSpec((ROW_TILE, INPROJ_TN), lambda i, j: (i, jnp.minimum(j, n_main - 1))),
            pl.BlockSpec((ROW_TILE, n_prec), lambda i, j: (i, 0)),
        ],
        out_shape=[jax.ShapeDtypeStruct((t, w_main.shape[1]), BF16),
                   jax.ShapeDtypeStruct((t, n_prec), F32)],
        scratch_shapes=[pltpu.VMEM((ROW_TILE, d), BF16), pltpu.VMEM((ROW_TILE, d), BF16)],
        compiler_params=_cparams(("arbitrary", "arbitrary")),
        name="inproj",
    )(x2, mod3, mod3, gain, w_main, w_hi, w_lo)


HALO = 16


def _conv_kernel(cb_ref, cc_ref, ch_ref, ccp_ref, chp_ref, w_ref, g_ref, o_ref, *, tiles_per_seq):
    i = pl.program_id(0)
    u = cc_ref[...].astype(F32) * ch_ref[...].astype(F32)
    up = ccp_ref[...].astype(F32) * chp_ref[...].astype(F32)
    up = jnp.where(i % tiles_per_seq == 0, 0.0, up)
    rows = lax.broadcasted_iota(jnp.int32, u.shape, 0)
    u1 = jnp.where(rows == 0, up[HALO - 1:HALO], pltpu.roll(u, 1, 0))
    u2 = jnp.where(rows == 0, up[HALO - 2:HALO - 1],
                   jnp.where(rows == 1, up[HALO - 1:HALO], pltpu.roll(u, 2, 0)))
    w = w_ref[...]
    y = cb_ref[...].astype(F32) * (w[2:3] * u + w[1:2] * u1 + w[0:1] * u2)
    gain = g_ref[...]
    for g in range(y.shape[1] // CONV_GROUP):
        sl = slice(g * CONV_GROUP, (g + 1) * CONV_GROUP)
        o_ref[:, sl] = _rms(y[:, sl], gain[:, sl]).astype(o_ref.dtype)


def _conv(pm, conv_w, gain, seq, cw):
    t = pm.shape[0]
    tiles_per_seq = seq // ROW_TILE
    hb = ROW_TILE // HALO

    def prev(col):
        return pl.BlockSpec((HALO, cw), lambda i: (jnp.maximum(i * hb - 1, 0), col))

    return pl.pallas_call(
        functools.partial(_conv_kernel, tiles_per_seq=tiles_per_seq),
        grid=(t // ROW_TILE,),
        in_specs=[pl.BlockSpec((ROW_TILE, cw), lambda i: (i, 0)),
                  pl.BlockSpec((ROW_TILE, cw), lambda i: (i, 1)),
                  pl.BlockSpec((ROW_TILE, cw), lambda i: (i, 2)),
                  prev(1), prev(2),
                  pl.BlockSpec((CONV_KSIZE, cw), lambda i: (0, 0)),
                  pl.BlockSpec((1, cw), lambda i: (0, 0))],
        out_specs=pl.BlockSpec((ROW_TILE, cw), lambda i: (i, 0)),
        out_shape=jax.ShapeDtypeStruct((t, cw), BF16),
        compiler_params=_cparams(("arbitrary",)),
        name="conv",
    )(pm, pm, pm, pm, pm, conv_w, gain)


def _rope_kernel(pos_ref, inv_ref, sign_ref, cos_ref, sin_ref):
    ang = pos_ref[...].astype(F32) * inv_ref[...]
    cos_ref[...] = jnp.cos(ang)
    sin_ref[...] = jnp.sin(ang) * sign_ref[...]


def _rope_tables(pos2, inv_full, sign):
    t = pos2.shape[0]
    ts = 2048
    return pl.pallas_call(
        _rope_kernel,
        grid=(t // ts,),
        in_specs=[pl.BlockSpec((ts, 1), lambda i: (i, 0)),
                  pl.BlockSpec((1, HEAD_DIM), lambda i: (0, 0)),
                  pl.BlockSpec((1, HEAD_DIM), lambda i: (0, 0))],
        out_specs=[pl.BlockSpec((ts, HEAD_DIM), lambda i: (i, 0))] * 2,
        out_shape=[jax.ShapeDtypeStruct((t, HEAD_DIM), F32)] * 2,
        compiler_params=_cparams(("arbitrary",)),
        name="rope",
    )(pos2, inv_full, sign)


def _prep_kernel(pp_ref, vc_ref, ksl_ref, vsl_ref, kw_ref, vw_ref, gl_ref, cos_ref, sin_ref,
                 qn_ref, kn_ref,
                 q_out, kc_out, vc_out, ksl_out, vsl_out, kw_out, vw_out, gate_out):
    cosf = cos_ref[...]
    sinf = sin_ref[...]
    scale = HEAD_DIM ** -0.5
    qn = qn_ref[...]
    kn = kn_ref[...]
    for h in range(N_HEADS):
        qh = pp_ref[:, h * HEAD_DIM:(h + 1) * HEAD_DIM]
        q_out[0, h] = _rope(_rms(qh, qn), cosf, sinf) * scale
    qw = N_HEADS * HEAD_DIM
    for g in range(N_KV):
        sl = slice(g * HEAD_DIM, (g + 1) * HEAD_DIM)
        kc_out[0, g] = pp_ref[:, qw + g * HEAD_DIM:qw + (g + 1) * HEAD_DIM]
        vc_out[0, g] = vc_ref[:, sl]
        vsl_out[0, g] = vsl_ref[:, sl]
        vw_out[0, g] = vw_ref[:, sl]
        ksl_out[0, g] = _rope(_rms(ksl_ref[:, sl].astype(F32), kn[1:2]), cosf, sinf).astype(BF16)
        kw_out[0, g] = _rope(_rms(kw_ref[:, sl].astype(F32), kn[2:3]), cosf, sinf).astype(BF16)
    gate_out[0] = jax.nn.sigmoid(gl_ref[...].astype(F32))


def _prep(pm, pp, cosf, sinf, q_norm, k_norm, batch, seq, kv_col0, gl_col):
    ts = ROW_TILE
    tiles = seq // ts
    kvw = N_KV * HEAD_DIM
    kvb = kv_col0 // kvw

    def rows(width, col):
        return pl.BlockSpec((ts, width), lambda b, s: (b * tiles + s, col))

    def heads(n):
        return pl.BlockSpec((1, n, ts, HEAD_DIM), lambda b, s: (b, 0, s, 0))

    kv_shape = jax.ShapeDtypeStruct((batch, N_KV, seq, HEAD_DIM), BF16)
    return pl.pallas_call(
        _prep_kernel,
        grid=(batch, tiles),
        in_specs=[rows(pp.shape[1], 0),
                  rows(kvw, kvb), rows(kvw, kvb + 1), rows(kvw, kvb + 2), rows(kvw, kvb + 3),
                  rows(kvw, kvb + 4),
                  rows(LANES, gl_col // LANES),
                  rows(HEAD_DIM, 0), rows(HEAD_DIM, 0),
                  pl.BlockSpec((1, HEAD_DIM), lambda b, s: (0, 0)),
                  pl.BlockSpec((3, HEAD_DIM), lambda b, s: (0, 0))],
        out_specs=[heads(N_HEADS), heads(N_KV), heads(N_KV), heads(N_KV), heads(N_KV), heads(N_KV),
                   heads(N_KV),
                   pl.BlockSpec((1, ts, LANES), lambda b, s: (b, s, 0))],
        out_shape=[jax.ShapeDtypeStruct((batch, N_HEADS, seq, HEAD_DIM), F32),
                   jax.ShapeDtypeStruct((batch, N_KV, seq, HEAD_DIM), F32),
                   kv_shape, kv_shape, kv_shape, kv_shape, kv_shape,
                   jax.ShapeDtypeStruct((batch, seq, LANES), F32)],
        compiler_params=_cparams(("arbitrary", "arbitrary")),
        name="prep",
    )(pp, pm, pm, pm, pm, pm, pm, cosf, sinf, q_norm, k_norm)


def _shift_up(h2, h2_next):
    n = h2.shape[0]
    rows = lax.broadcasted_iota(jnp.int32, h2.shape, 0)
    return jnp.where(rows == n - 1, h2_next[0:1], pltpu.roll(h2, n - 1, 0))


def _compress_kernel(ak_ref, akn_ref, av_ref, avn_ref,
                     pek_ref, wka_hi, wka_lo, wkb_hi, wkb_lo, bk_ref, wk2_hi, wk2_lo,
                     pev_ref, wva, wvb, bv_ref, wv2,
                     kn_ref, cos_ref, sin_ref,
                     kh_out, kl_out, v_out):
    a_hi, a_lo = _split(ak_ref[0])
    n_hi, n_lo = _split(akn_ref[0])
    p_hi, p_lo = _split(pek_ref[...])
    h1 = _dot3(a_hi, a_lo, wka_hi[...], wka_lo[...])
    h2 = _dot3(a_hi, a_lo, wkb_hi[...], wkb_lo[...])
    h2n = _dot3(n_hi, n_lo, wkb_hi[...], wkb_lo[...])
    pe = (_dot3(p_hi, p_lo, wka_hi[...], wka_lo[...])[0:1]
          + _dot3(p_hi, p_lo, wkb_hi[...], wkb_lo[...])[1:2])
    hid = jax.nn.gelu(h1 + _shift_up(h2, h2n) + pe + bk_ref[...], approximate=True)
    hid_hi, hid_lo = _split(hid)
    kc = _dot3(hid_hi, hid_lo, wk2_hi[...], wk2_lo[...])
    kc = _rope(_rms(kc, kn_ref[0:1]), cos_ref[0], sin_ref[0])
    k_hi, k_lo = _split(kc)
    kh_out[0] = k_hi
    kl_out[0] = k_lo

    av = av_ref[0]
    pv = pev_ref[...].astype(BF16)
    g1 = _dot(av, wva[...])
    g2 = _dot(av, wvb[...])
    g2n = _dot(avn_ref[0], wvb[...])
    pev = _dot(pv, wva[...])[0:1] + _dot(pv, wvb[...])[1:2]
    hv = jax.nn.gelu(g1 + _shift_up(g2, g2n) + pev + bv_ref[...], approximate=True)
    v_out[0] = _dot(hv.astype(BF16), wv2[...]).astype(BF16)


def _compress(ak, av, pek2, wk, bk, wk2, pev2, wv, bv, wv2, k_norm, cosc, sinc):
    bg, nh, kk = ak.shape
    hid = bk.shape[1]
    tiles = nh // CMP_TILE
    cur = pl.BlockSpec((1, CMP_TILE, kk), lambda g, t: (g, t, 0))

    def nxt(rows):
        per = CMP_TILE // rows
        last = nh // rows - 1
        return pl.BlockSpec((1, rows, kk), lambda g, t: (g, jnp.minimum((t + 1) * per, last), 0))

    full = lambda shp: pl.BlockSpec(shp, lambda g, t: tuple(0 for _ in shp))
    w1 = full((kk, hid))
    out = pl.BlockSpec((1, CMP_TILE, HEAD_DIM), lambda g, t: (g, t, 0))
    tab = pl.BlockSpec((1, CMP_TILE, HEAD_DIM), lambda g, t: (g // N_KV, t, 0))
    oshape = jax.ShapeDtypeStruct((bg, nh, HEAD_DIM), BF16)
    return pl.pallas_call(
        _compress_kernel,
        grid=(bg, tiles),
        in_specs=[cur, nxt(8), cur, nxt(16),
                  full((8, kk)), w1, w1, w1, w1, full((1, hid)), full((hid, HEAD_DIM)),
                  full((hid, HEAD_DIM)),
                  full((8, kk)), w1, w1, full((1, hid)), full((hid, HEAD_DIM)),
                  full((3, HEAD_DIM)), tab, tab],
        out_specs=[out, out, out],
        out_shape=[oshape, oshape, oshape],
        compiler_params=_cparams(("arbitrary", "arbitrary")),
        name="compress",
    )(ak, ak, av, av, pek2, *wk, bk, *wk2, pev2, *wv, bv, wv2, k_norm, cosc, sinc)


def _count(mask):
    return jnp.sum(jnp.where(mask, 1.0, 0.0), axis=-1, keepdims=True)


def _select_blocks(imp, t_pos, n_lanes):
    j = lax.broadcasted_iota(jnp.int32, (1, n_lanes), 1)
    cur = jnp.right_shift(t_pos, SEL_BLOCK.bit_length() - 1)
    valid = j * SEL_BLOCK <= t_pos
    forced = (j == 0) | (j == cur) | (j == cur - 1)
    cand = valid & jnp.logical_not(forced)
    n_forced = 1.0 + jnp.where(cur >= 1, 1.0, 0.0) + jnp.where(cur >= 2, 1.0, 0.0)
    k = float(SEL_TOP_N) - n_forced

    def value_step(it, tb):
        trial = tb | jnp.left_shift(jnp.int32(1), 30 - it)
        keep = _count(cand & (imp >= pltpu.bitcast(trial, F32))) >= k
        return jnp.where(keep, trial, tb)

    tb = lax.fori_loop(0, 31, value_step, jnp.zeros(t_pos.shape, jnp.int32))
    thr = pltpu.bitcast(tb, F32)
    above = cand & (imp > thr)
    tied = cand & (imp == thr)
    need = k - _count(above)

    def index_step(it, jb):
        trial = jb | jnp.left_shift(jnp.int32(1), (n_lanes.bit_length() - 1) - it)
        keep = _count(tied & (j < trial)) < need
        return jnp.where(keep, trial, jb)

    jb = lax.fori_loop(0, n_lanes.bit_length(), index_step, jnp.zeros(t_pos.shape, jnp.int32))
    return (forced & valid) | above | (tied & (j <= jb))


def _attn_kernel(q_ref, kch_ref, kcl_ref, vc_ref, ksl_ref, vsl_ref, kw_ref, vw_ref,
                 gate_ref, gn_ref, ov_ref, o_ref, qa_ref, m_ref, l_ref, acc_ref):
    gi = pl.program_id(1)
    i = pl.program_id(2)
    tq = q_ref.shape[2]
    rows = GROUP_R * tq
    nc = kch_ref.shape[1]
    nsp = ov_ref.shape[1]
    ck = SEL_CHUNK
    blocks_per_chunk = ck // SEL_BLOCK

    q = q_ref[0].reshape(rows, HEAD_DIM)
    q_hi, q_lo = _split(q)
    t_row = i * tq + jnp.bitwise_and(lax.broadcasted_iota(jnp.int32, (rows, 1), 0), tq - 1)

    s = _dot3(q_hi, q_lo, kch_ref[0], kcl_ref[0], dot=_dot_nt)
    n_end = lax.broadcasted_iota(jnp.int32, (1, nc), 1) * CMP_STRIDE + (CMP_BLOCK - 1)
    vis = n_end <= t_row
    s = jnp.where(vis, s, NEG_INF)
    p = jnp.where(vis, jnp.exp(s - jnp.max(s, axis=-1, keepdims=True)), 0.0)
    l = jnp.sum(p, axis=-1, keepdims=True)
    some = l > 0.0
    pc = p * jnp.where(some, 1.0 / jnp.where(some, l, 1.0), 0.0)
    o_cmp = _dot(pc.astype(BF16), vc_ref[0])

    psum = pc[0:tq]
    for r in range(1, GROUP_R):
        psum = psum + pc[r * tq:(r + 1) * tq]
    ps_hi, ps_lo = _split(psum)
    ov = ov_ref[...]
    imp = _dot(ps_hi, ov) + _dot(ps_lo, ov)
    t_pos = i * tq + lax.broadcasted_iota(jnp.int32, (tq, 1), 0)
    sel = _select_blocks(imp, t_pos, nsp)
    selneg = jnp.where(sel, 0.0, NEG_INF).astype(BF16)
    selneg = jnp.concatenate([selneg] * GROUP_R, axis=0)
    for w in range(nsp // LANES):
        qa_ref[w] = jnp.concatenate([q_hi, selneg[:, w * LANES:(w + 1) * LANES]], axis=1)

    m_ref[...] = jnp.full(m_ref.shape, NEG_INF, F32)
    l_ref[...] = jnp.zeros(l_ref.shape, F32)
    acc_ref[...] = jnp.zeros(acc_ref.shape, F32)

    def sel_chunk(c, causal):
        k0 = pl.multiple_of(c * ck, ck)
        kblk = c * blocks_per_chunk + jnp.right_shift(
            lax.broadcasted_iota(jnp.int32, (ck, 1), 0), SEL_BLOCK.bit_length() - 1)
        lane = lax.broadcasted_iota(jnp.int32, (1, LANES), 1)
        onehot = jnp.where(jnp.bitwise_and(kblk, LANES - 1) == lane, 1.0, 0.0).astype(BF16)
        ka = jnp.concatenate([ksl_ref[0, 0, pl.ds(k0, ck), :], onehot], axis=1)
        sc = _dot_nt(qa_ref[c // (LANES // blocks_per_chunk)], ka)
        if causal:
            kp = c * ck + lax.broadcasted_iota(jnp.int32, (1, ck), 1)
            sc = jnp.where(kp <= t_row, sc, NEG_INF)
        m_old = m_ref[...]
        m_new = jnp.maximum(m_old, jnp.max(sc, axis=-1, keepdims=True))
        alpha = jnp.exp(m_old - m_new)
        pe = jnp.exp(sc - m_new)
        l_ref[...] = alpha * l_ref[...] + jnp.sum(pe, axis=-1, keepdims=True)
        acc_ref[...] = alpha * acc_ref[...] + _dot(pe.astype(BF16), vsl_ref[0, 0, pl.ds(k0, ck), :])
        m_ref[...] = m_new

    c_last = (i * tq) // ck

    def past_chunk(c, carry):
        sel_chunk(c, causal=False)
        return carry

    lax.fori_loop(0, c_last, past_chunk, 0)
    sel_chunk(c_last, causal=True)
    o_slc = acc_ref[...] / l_ref[...]

    wk = WINDOW + tq
    start = pl.multiple_of(jnp.maximum(i * tq - WINDOW, 0), tq)
    sw = _dot_nt(q_hi, kw_ref[0, 0, pl.ds(start, wk), :])
    kp = start + lax.broadcasted_iota(jnp.int32, (1, wk), 1)
    sw = jnp.where((kp <= t_row) & (kp > t_row - WINDOW), sw, NEG_INF)
    pw = jnp.exp(sw - jnp.max(sw, axis=-1, keepdims=True))
    o_win = _dot(pw.astype(BF16), vw_ref[0, 0, pl.ds(start, wk), :]) / jnp.sum(pw, axis=-1, keepdims=True)

    gates = gate_ref[0]
    gn = gn_ref[...]
    n_gate = 3 * GROUP_R
    for r in range(GROUP_R):
        sl = slice(r * tq, (r + 1) * tq)

        def gate(branch, r=r):
            col = 3 * r + branch
            return jnp.where(gi == 0, gates[:, col:col + 1], gates[:, n_gate + col:n_gate + col + 1])

        o = gate(0) * o_cmp[sl] + gate(1) * o_slc[sl] + gate(2) * o_win[sl]
        hs = slice(r * HEAD_DIM, (r + 1) * HEAD_DIM)
        o_ref[0, :, hs] = _rms(o, gn[:, hs]).astype(o_ref.dtype)


def _attn(q_r, kc_hi, kc_lo, v_cmp, ksl, vsl, kw, vw, gates, gain, ov):
    batch, _, seq, _ = q_r.shape
    nc = kc_hi.shape[1]
    nsp = ov.shape[1]
    tq = Q_TILE
    rows = GROUP_R * tq
    gw = GROUP_R * HEAD_DIM
    cmp_spec = pl.BlockSpec((1, nc, HEAD_DIM), lambda b, g, i: (b * N_KV + g, 0, 0))
    kv_spec = pl.BlockSpec((1, 1, seq, HEAD_DIM), lambda b, g, i: (b, g, 0, 0))
    return pl.pallas_call(
        _attn_kernel,
        grid=(batch, N_KV, seq // tq),
        in_specs=[pl.BlockSpec((1, GROUP_R, tq, HEAD_DIM), lambda b, g, i: (b, g, i, 0)),
                  cmp_spec, cmp_spec, cmp_spec,
                  kv_spec, kv_spec, kv_spec, kv_spec,
                  pl.BlockSpec((1, tq, LANES), lambda b, g, i: (b, i, 0)),
                  pl.BlockSpec((1, gw), lambda b, g, i: (0, g)),
                  pl.BlockSpec((nc, nsp), lambda b, g, i: (0, 0))],
        out_specs=pl.BlockSpec((1, tq, gw), lambda b, g, i: (b, i, g)),
        out_shape=jax.ShapeDtypeStruct((batch, seq, N_KV * gw), BF16),
        scratch_shapes=[pltpu.VMEM((nsp // LANES, rows, 2 * HEAD_DIM), BF16),
                        pltpu.VMEM((rows, 1), F32), pltpu.VMEM((rows, 1), F32),
                        pltpu.VMEM((rows, HEAD_DIM), F32)],
        compiler_params=_cparams(("arbitrary", "arbitrary", "arbitrary")),
        name="attn",
    )(q_r, kc_hi, kc_lo, v_cmp, ksl, vsl, kw, vw, gates, gain, ov)


def _outproj_kernel(yc_ref, ya_ref, x_ref, w_ref, ga_ref, g_ref, sh_ref, sc_ref, x1_ref, h2_ref):
    cw = yc_ref.shape[1]
    acc = _dot(yc_ref[...], w_ref[0:cw, :]) + _dot(ya_ref[...], w_ref[cw:, :])
    x1 = x_ref[...] + ga_ref[0] * acc
    x1_ref[...] = x1
    h2_ref[...] = (_rms(x1, g_ref[...]) * (1.0 + sc_ref[0]) + sh_ref[0]).astype(h2_ref.dtype)


def _outproj(y_conv, y_attn, x2, w_out, mod3, gain, seq):
    t, d = x2.shape
    tiles_per_seq = seq // ROW_TILE
    mod = lambda k: pl.BlockSpec((1, 1, d), lambda i: (i // tiles_per_seq, 0, k))
    row = lambda w: pl.BlockSpec((ROW_TILE, w), lambda i: (i, 0))
    return pl.pallas_call(
        _outproj_kernel,
        grid=(t // ROW_TILE,),
        in_specs=[row(y_conv.shape[1]), row(y_attn.shape[1]), row(d),
                  pl.BlockSpec(w_out.shape, lambda i: (0, 0)),
                  mod(2), pl.BlockSpec((1, d), lambda i: (0, 0)), mod(3), mod(4)],
        out_specs=[row(d), row(d)],
        out_shape=[jax.ShapeDtypeStruct((t, d), F32), jax.ShapeDtypeStruct((t, d), BF16)],
        compiler_params=_cparams(("arbitrary",)),
        name="outproj",
    )(y_conv, y_attn, x2, w_out, mod3, gain, mod3, mod3)


def _ffn_kernel(h_ref, w1_ref, w3_ref, w2_ref, x1_ref, gf_ref, o_ref, acc_ref):
    f = pl.program_id(1)
    h = h_ref[...]
    act = (jax.nn.silu(_dot(h, w1_ref[...])) * _dot(h, w3_ref[...])).astype(BF16)
    part = _dot(act, w2_ref[...])

    @pl.when(f == 0)
    def _():
        acc_ref[...] = part

    @pl.when(f > 0)
    def _():
        acc_ref[...] += part

    @pl.when(f == pl.num_programs(1) - 1)
    def _():
        o_ref[...] = x1_ref[...] + gf_ref[0] * acc_ref[...]


def _ffn(h2, w1, w3, w2, x1, mod3, seq):
    t, d = x1.shape
    dff = w1.shape[1]
    tiles_per_seq = seq // ROW_TILE
    return pl.pallas_call(
        _ffn_kernel,
        grid=(t // ROW_TILE, dff // FFN_TF),
        in_specs=[pl.BlockSpec((ROW_TILE, d), lambda i, f: (i, 0)),
                  pl.BlockSpec((d, FFN_TF), lambda i, f: (0, f)),
                  pl.BlockSpec((d, FFN_TF), lambda i, f: (0, f)),
                  pl.BlockSpec((FFN_TF, d), lambda i, f: (f, 0)),
                  pl.BlockSpec((ROW_TILE, d), lambda i, f: (i, 0)),
                  pl.BlockSpec((1, 1, d), lambda i, f: (i // tiles_per_seq, 0, 5))],
        out_specs=pl.BlockSpec((ROW_TILE, d), lambda i, f: (i, 0)),
        out_shape=jax.ShapeDtypeStruct((t, d), F32),
        scratch_shapes=[pltpu.VMEM((ROW_TILE, d), F32)],
        compiler_params=_cparams(("arbitrary", "arbitrary")),
        name="ffn",
    )(h2, w1, w3, w2, x1, mod3)


def _overlap_matrix(n_cmp_pad, n_sel, n_sel_pad):
    cmp_start = np.arange(n_cmp_pad) * CMP_STRIDE
    cmp_end = cmp_start + CMP_BLOCK - 1
    sel_start = np.arange(n_sel) * SEL_BLOCK
    ov = ((cmp_start[:, None] < sel_start[None, :] + SEL_BLOCK)
          & (cmp_end[:, None] >= sel_start[None, :])).astype(np.float32)
    return np.pad(ov, ((0, 0), (0, n_sel_pad - n_sel)))


def _w1_halves(w1):
    l, dk, hid = w1.shape
    half = l // 2
    return w1[:half].reshape(half * dk, hid), w1[half:].reshape(half * dk, hid)


def _pe_rows(pe):
    l, dk = pe.shape
    half = l // 2
    rows = jnp.stack([pe[:half].reshape(half * dk), pe[half:].reshape(half * dk)])
    return jnp.pad(rows, ((0, 6), (0, 0)))


def kernel(x, c, positions, ada_w, ada_b, norm_mix, norm_ffn, w_in, conv_w, cmp_pe_k, cmp_k_w1, cmp_k_b1, cmp_k_w2, cmp_pe_v, cmp_v_w1, cmp_v_b1, cmp_v_w2, q_norm, k_norm, out_norm_conv, out_norm_attn, w_out, ffn_w1, ffn_w3, ffn_w2):
    batch, seq, d = x.shape
    t = batch * seq
    depth = ada_w.shape[0]
    cw = conv_w.shape[2]
    aw = N_HEADS * HEAD_DIM
    kvw = N_KV * HEAD_DIM
    n_gate = 3 * N_HEADS
    assert seq % max(ROW_TILE, SEL_CHUNK) == 0 and seq >= WINDOW + Q_TILE
    assert seq // SEL_BLOCK >= SEL_TOP_N and CMP_BLOCK == 2 * CMP_STRIDE
    nh = seq // CMP_STRIDE
    n_sel = seq // SEL_BLOCK
    n_sel_pad = -(-n_sel // LANES) * LANES
    n_sel_pad = 1 << (n_sel_pad - 1).bit_length()
    assert nh % CMP_TILE == 0

    inv = 1.0 / (ROPE_THETA ** (jnp.arange(0, HEAD_DIM, 2, dtype=F32) / HEAD_DIM))
    inv_full = jnp.concatenate([inv, inv])[None, :]
    sign = jnp.concatenate([-jnp.ones(HEAD_DIM // 2, F32), jnp.ones(HEAD_DIM // 2, F32)])[None, :]
    cosf, sinf = _rope_tables(positions.reshape(t, 1), inv_full, sign)
    at_end = lambda tab: jnp.pad(
        tab.reshape(batch, nh, CMP_STRIDE, HEAD_DIM)[:, 1:, CMP_STRIDE - 1], ((0, 0), (0, 1), (0, 0)))
    cosc, sinc = at_end(cosf), at_end(sinf)
    ov = jnp.asarray(_overlap_matrix(nh, n_sel, n_sel_pad), BF16)

    c8 = jnp.pad(c, ((0, 8 - batch), (0, 0)))
    x2 = x.reshape(t, d)
    for l in range(depth):
        mod = _ada(c8, ada_w[l], ada_b[l][None, :])
        mod3 = mod[:batch].reshape(batch, 1, 6 * d)

        w = w_in[l]
        o_q = 3 * cw
        o_kc = o_q + aw
        o_rest = o_kc + kvw
        o_gl = o_rest + 5 * kvw
        main_w = o_q + 5 * kvw + n_gate
        main_pad = -(-main_w // INPROJ_TN) * INPROJ_TN
        w_main = jnp.concatenate(
            [w[:, :o_q], w[:, o_rest:o_gl], w[:, o_gl:o_gl + n_gate],
             jnp.zeros((d, main_pad - main_w), F32)], axis=1).astype(BF16)
        w_hi, w_lo = _split(w[:, o_q:o_rest])
        pm, pp = _inproj(x2, mod3, norm_mix[l][None, :], w_main, w_hi, w_lo, seq)

        y_conv = _conv(pm, conv_w[l], out_norm_conv[l][None, :], seq, cw)

        q_r, kc_t, vc_t, ksl, vsl, kw, vw, gates = _prep(
            pm, pp, cosf, sinf, q_norm[l][None, :], k_norm[l], batch, seq, o_q, o_q + 5 * kvw)

        wka, wkb = _w1_halves(cmp_k_w1[l])
        wva, wvb = _w1_halves(cmp_v_w1[l])
        kc_hi, kc_lo, v_cmp = _compress(
            kc_t.reshape(batch * N_KV, nh, CMP_STRIDE * HEAD_DIM),
            vc_t.reshape(batch * N_KV, nh, CMP_STRIDE * HEAD_DIM),
            _pe_rows(cmp_pe_k[l]), (*_split(wka), *_split(wkb)), cmp_k_b1[l][None, :],
            _split(cmp_k_w2[l]),
            _pe_rows(cmp_pe_v[l]), (wva.astype(BF16), wvb.astype(BF16)), cmp_v_b1[l][None, :],
            cmp_v_w2[l].astype(BF16), k_norm[l], cosc, sinc)

        y_attn = _attn(q_r, kc_hi, kc_lo, v_cmp, ksl, vsl, kw, vw, gates,
                       out_norm_attn[l][None, :], ov)

        x1, h2 = _outproj(y_conv, y_attn.reshape(t, aw), x2, w_out[l].astype(BF16), mod3,
                          norm_ffn[l][None, :], seq)
        x2 = _ffn(h2, ffn_w1[l].astype(BF16), ffn_w3[l].astype(BF16), ffn_w2[l].astype(BF16),
                  x1, mod3, seq)
    return x2.reshape(batch, seq, d)
```

```python
import functools

import numpy as np
import jax
import jax.numpy as jnp
from jax import lax
from jax.experimental import pallas as pl
from jax.experimental.pallas import tpu as pltpu

F32 = jnp.float32
BF16 = jnp.bfloat16

CONV_GROUP = 128
CONV_KSIZE = 3
HEAD_DIM = 128
N_KV = 2
GROUP_R = 4
N_HEADS = N_KV * GROUP_R
CMP_BLOCK = 32
CMP_STRIDE = 16
SEL_BLOCK = 64
SEL_TOP_N = 16
WINDOW = 512
ROPE_THETA = 10000.0
EPS = 1e-6
NEG_INF = -1e30
LOG2E = 1.4426950408889634
SHIFT_MARGIN = 57.0
SAFE_BOUND = 86.0

LANES = 128
VMEM_LIMIT = 56 * 1024 * 1024

ROW_TILE = 512
Q_TILE = 128
SEL_CHUNK = 512
CMP_TILE = 256
ADA_TN = 1024
INPROJ_TN = 768
FFN_TF = 512


def _cparams(sem):
    return pltpu.CompilerParams(dimension_semantics=sem, vmem_limit_bytes=VMEM_LIMIT)


def _split(a):
    hi = a.astype(BF16)
    lo = (a - hi.astype(F32)).astype(BF16)
    return hi, lo


def _dot(a, b):
    return jnp.dot(a, b, preferred_element_type=F32)


def _dot_nt(a, b):
    return lax.dot_general(a, b, (((1,), (1,)), ((), ())), preferred_element_type=F32)


def _dot3(a_hi, a_lo, b_hi, b_lo, dot=_dot):
    return (dot(a_hi, b_lo) + dot(a_lo, b_hi)) + dot(a_hi, b_hi)


def _rms(x, gain):
    return x * lax.rsqrt(jnp.mean(x * x, axis=-1, keepdims=True) + EPS) * gain


def _rope(y, cosf, sinf):
    return y * cosf + pltpu.roll(y, HEAD_DIM // 2, 1) * sinf


def _ada_kernel(c_ref, w_ref, b_ref, o_ref):
    s_hi, s_lo = _split(jax.nn.silu(c_ref[...]))
    w_hi, w_lo = _split(w_ref[...])
    o_ref[...] = _dot3(s_hi, s_lo, w_hi, w_lo) + b_ref[...]


def _ada(c8, w, b):
    d, n = w.shape
    return pl.pallas_call(
        _ada_kernel,
        grid=(n // ADA_TN,),
        in_specs=[pl.BlockSpec((8, d), lambda j: (0, 0)),
                  pl.BlockSpec((d, ADA_TN), lambda j: (0, j)),
                  pl.BlockSpec((1, ADA_TN), lambda j: (0, j))],
        out_specs=pl.BlockSpec((8, ADA_TN), lambda j: (0, j)),
        out_shape=jax.ShapeDtypeStruct((8, n), F32),
        compiler_params=_cparams(("arbitrary",)),
        name="ada",
    )(c8, w, b)


def _inproj_kernel(x_ref, sh_ref, sc_ref, g_ref, wm_ref, whi_ref, wlo_ref,
                   om_ref, op_ref, hhi_ref, hlo_ref, *, n_main):
    j = pl.program_id(1)

    @pl.when(j == 0)
    def _():
        h = _rms(x_ref[...], g_ref[...]) * (1.0 + sc_ref[0]) + sh_ref[0]
        hi, lo = _split(h)
        hhi_ref[...] = hi
        hlo_ref[...] = lo

    @pl.when(j < n_main)
    def _():
        om_ref[...] = _dot(hhi_ref[...], wm_ref[...]).astype(om_ref.dtype)

    @pl.when(j == n_main)
    def _():
        op_ref[...] = _dot3(hhi_ref[...], hlo_ref[...], whi_ref[...], wlo_ref[...])


def _inproj(x2, mod3, gain, w_main, w_hi, w_lo, seq):
    t, d = x2.shape
    n_main = w_main.shape[1] // INPROJ_TN
    n_prec = w_hi.shape[1]
    tiles_per_seq = seq // ROW_TILE
    return pl.pallas_call(
        functools.partial(_inproj_kernel, n_main=n_main),
        grid=(t // ROW_TILE, n_main + 1),
        in_specs=[
            pl.BlockSpec((ROW_TILE, d), lambda i, j: (i, 0)),
            pl.BlockSpec((1, 1, d), lambda i, j: (i // tiles_per_seq, 0, 0)),
            pl.BlockSpec((1, 1, d), lambda i, j: (i // tiles_per_seq, 0, 1)),
            pl.BlockSpec((1, d), lambda i, j: (0, 0)),
            pl.BlockSpec((d, INPROJ_TN), lambda i, j: (0, jnp.minimum(j, n_main - 1))),
            pl.BlockSpec((d, n_prec), lambda i, j: (0, 0)),
            pl.BlockSpec((d, n_prec), lambda i, j: (0, 0)),
        ],
        out_specs=[
            pl.BlockSpec((ROW_TILE, INPROJ_TN), lambda i, j: (i, jnp.minimum(j, n_main - 1))),
            pl.BlockSpec((ROW_TILE, n_prec), lambda i, j: (i, 0)),
        ],
        out_shape=[jax.ShapeDtypeStruct((t, w_main.shape[1]), BF16),
                   jax.ShapeDtypeStruct((t, n_prec), F32)],
        scratch_shapes=[pltpu.VMEM((ROW_TILE, d), BF16), pltpu.VMEM((ROW_TILE, d), BF16)],
        compiler_params=_cparams(("arbitrary", "arbitrary")),
        name="inproj",
    )(x2, mod3, mod3, gain, w_main, w_hi, w_lo)


HALO = 16


def _conv_kernel(cb_ref, cc_ref, ch_ref, ccp_ref, chp_ref, w_ref, g_ref, o_ref, *, tiles_per_seq):
    i = pl.program_id(0)
    u = cc_ref[...].astype(F32) * ch_ref[...].astype(F32)
    up = ccp_ref[...].astype(F32) * chp_ref[...].astype(F32)
    up = jnp.where(i % tiles_per_seq == 0, 0.0, up)
    rows = lax.broadcasted_iota(jnp.int32, u.shape, 0)
    u1 = jnp.where(rows == 0, up[HALO - 1:HALO], pltpu.roll(u, 1, 0))
    u2 = jnp.where(rows == 0, up[HALO - 2:HALO - 1],
                   jnp.where(rows == 1, up[HALO - 1:HALO], pltpu.roll(u, 2, 0)))
    w = w_ref[...]
    y = cb_ref[...].astype(F32) * (w[2:3] * u + w[1:2] * u1 + w[0:1] * u2)
    gain = g_ref[...]
    for g in range(y.shape[1] // CONV_GROUP):
        sl = slice(g * CONV_GROUP, (g + 1) * CONV_GROUP)
        o_ref[:, sl] = _rms(y[:, sl], gain[:, sl]).astype(o_ref.dtype)


def _conv(pm, conv_w, gain, seq, cw):
    t = pm.shape[0]
    tiles_per_seq = seq // ROW_TILE
    hb = ROW_TILE // HALO

    def prev(col):
        return pl.BlockSpec((HALO, cw), lambda i: (jnp.maximum(i * hb - 1, 0), col))

    return pl.pallas_call(
        functools.partial(_conv_kernel, tiles_per_seq=tiles_per_seq),
        grid=(t // ROW_TILE,),
        in_specs=[pl.BlockSpec((ROW_TILE, cw), lambda i: (i, 0)),
                  pl.BlockSpec((ROW_TILE, cw), lambda i: (i, 1)),
                  pl.BlockSpec((ROW_TILE, cw), lambda i: (i, 2)),
                  prev(1), prev(2),
                  pl.BlockSpec((CONV_KSIZE, cw), lambda i: (0, 0)),
                  pl.BlockSpec((1, cw), lambda i: (0, 0))],
        out_specs=pl.BlockSpec((ROW_TILE, cw), lambda i: (i, 0)),
        out_shape=jax.ShapeDtypeStruct((t, cw), BF16),
        compiler_params=_cparams(("arbitrary",)),
        name="conv",
    )(pm, pm, pm, pm, pm, conv_w, gain)


def _rope_kernel(pos_ref, inv_ref, sign_ref, cos_ref, sin_ref):
    ang = pos_ref[...].astype(F32) * inv_ref[...]
    cos_ref[...] = jnp.cos(ang)
    sin_ref[...] = jnp.sin(ang) * sign_ref[...]


def _rope_tables(pos2, inv_full, sign):
    t = pos2.shape[0]
    ts = 2048
    return pl.pallas_call(
        _rope_kernel,
        grid=(t // ts,),
        in_specs=[pl.BlockSpec((ts, 1), lambda i: (i, 0)),
                  pl.BlockSpec((1, HEAD_DIM), lambda i: (0, 0)),
                  pl.BlockSpec((1, HEAD_DIM), lambda i: (0, 0))],
        out_specs=[pl.BlockSpec((ts, HEAD_DIM), lambda i: (i, 0))] * 2,
        out_shape=[jax.ShapeDtypeStruct((t, HEAD_DIM), F32)] * 2,
        compiler_params=_cparams(("arbitrary",)),
        name="rope",
    )(pos2, inv_full, sign)


def _prep_kernel(pp_ref, vc_ref, ksl_ref, vsl_ref, kw_ref, vw_ref, gl_ref, cos_ref, sin_ref,
                 qn_ref, kn_ref,
                 q_out, kc_out, vc_out, ksl_out, vsl_out, kw_out, vw_out, gate_out, kmax_out):
    cosf = cos_ref[...]
    sinf = sin_ref[...]
    scale = HEAD_DIM ** -0.5 * LOG2E
    qn = qn_ref[...]
    kn = kn_ref[...]
    for h in range(N_HEADS):
        qh = pp_ref[:, h * HEAD_DIM:(h + 1) * HEAD_DIM]
        q_out[0, h] = _rope(_rms(qh, qn), cosf, sinf) * scale
    qw = N_HEADS * HEAD_DIM
    for g in range(N_KV):
        sl = slice(g * HEAD_DIM, (g + 1) * HEAD_DIM)
        kc_out[0, g] = pp_ref[:, qw + g * HEAD_DIM:qw + (g + 1) * HEAD_DIM]
        vc_out[0, g] = vc_ref[:, sl]
        vsl_out[0, g] = vsl_ref[:, sl]
        vw_out[0, g] = vw_ref[:, sl]
        ksl = _rope(_rms(ksl_ref[:, sl].astype(F32), kn[1:2]), cosf, sinf).astype(BF16)
        ksl_out[0, g] = ksl
        kw_out[0, g] = _rope(_rms(kw_ref[:, sl].astype(F32), kn[2:3]), cosf, sinf).astype(BF16)
        kf = ksl.astype(F32)
        n2 = jnp.full(kmax_out.shape[2:], jnp.max(jnp.sum(kf * kf, axis=-1, keepdims=True)), F32)

        @pl.when(pl.program_id(1) == 0)
        def _(g=g, n2=n2):
            kmax_out[0, g] = n2

        @pl.when(pl.program_id(1) > 0)
        def _(g=g, n2=n2):
            kmax_out[0, g] = jnp.maximum(kmax_out[0, g], n2)
    gate_out[0] = jax.nn.sigmoid(gl_ref[...].astype(F32))


def _prep(pm, pp, cosf, sinf, q_norm, k_norm, batch, seq, kv_col0, gl_col):
    ts = ROW_TILE
    tiles = seq // ts
    kvw = N_KV * HEAD_DIM
    kvb = kv_col0 // kvw

    def rows(width, col):
        return pl.BlockSpec((ts, width), lambda b, s: (b * tiles + s, col))

    def heads(n):
        return pl.BlockSpec((1, n, ts, HEAD_DIM), lambda b, s: (b, 0, s, 0))

    kv_shape = jax.ShapeDtypeStruct((batch, N_KV, seq, HEAD_DIM), BF16)
    return pl.pallas_call(
        _prep_kernel,
        grid=(batch, tiles),
        in_specs=[rows(pp.shape[1], 0),
                  rows(kvw, kvb), rows(kvw, kvb + 1), rows(kvw, kvb + 2), rows(kvw, kvb + 3),
                  rows(kvw, kvb + 4),
                  rows(LANES, gl_col // LANES),
                  rows(HEAD_DIM, 0), rows(HEAD_DIM, 0),
                  pl.BlockSpec((1, HEAD_DIM), lambda b, s: (0, 0)),
                  pl.BlockSpec((3, HEAD_DIM), lambda b, s: (0, 0))],
        out_specs=[heads(N_HEADS), heads(N_KV), heads(N_KV), heads(N_KV), heads(N_KV), heads(N_KV),
                   heads(N_KV),
                   pl.BlockSpec((1, ts, LANES), lambda b, s: (b, s, 0)),
                   pl.BlockSpec((1, N_KV, 8, LANES), lambda b, s: (b, 0, 0, 0))],
        out_shape=[jax.ShapeDtypeStruct((batch, N_HEADS, seq, HEAD_DIM), F32),
                   jax.ShapeDtypeStruct((batch, N_KV, seq, HEAD_DIM), F32),
                   kv_shape, kv_shape, kv_shape, kv_shape, kv_shape,
                   jax.ShapeDtypeStruct((batch, seq, LANES), F32),
                   jax.ShapeDtypeStruct((batch, N_KV, 8, LANES), F32)],
        compiler_params=_cparams(("arbitrary", "arbitrary")),
        name="prep",
    )(pp, pm, pm, pm, pm, pm, pm, cosf, sinf, q_norm, k_norm)


def _shift_up(h2, h2_next):
    n = h2.shape[0]
    rows = lax.broadcasted_iota(jnp.int32, h2.shape, 0)
    return jnp.where(rows == n - 1, h2_next[0:1], pltpu.roll(h2, n - 1, 0))


def _compress_kernel(ak_ref, akn_ref, av_ref, avn_ref,
                     pek_ref, wka_hi, wka_lo, wkb_hi, wkb_lo, bk_ref, wk2_hi, wk2_lo,
                     pev_ref, wva, wvb, bv_ref, wv2,
                     kn_ref, cos_ref, sin_ref,
                     kh_out, kl_out, v_out):
    a_hi, a_lo = _split(ak_ref[0])
    n_hi, n_lo = _split(akn_ref[0])
    p_hi, p_lo = _split(pek_ref[...])
    h1 = _dot3(a_hi, a_lo, wka_hi[...], wka_lo[...])
    h2 = _dot3(a_hi, a_lo, wkb_hi[...], wkb_lo[...])
    h2n = _dot3(n_hi, n_lo, wkb_hi[...], wkb_lo[...])
    pe = (_dot3(p_hi, p_lo, wka_hi[...], wka_lo[...])[0:1]
          + _dot3(p_hi, p_lo, wkb_hi[...], wkb_lo[...])[1:2])
    hid = jax.nn.gelu(h1 + _shift_up(h2, h2n) + pe + bk_ref[...], approximate=True)
    hid_hi, hid_lo = _split(hid)
    kc = _dot3(hid_hi, hid_lo, wk2_hi[...], wk2_lo[...])
    kc = _rope(_rms(kc, kn_ref[0:1]), cos_ref[0], sin_ref[0])
    k_hi, k_lo = _split(kc)
    kh_out[0] = k_hi
    kl_out[0] = k_lo

    av = av_ref[0]
    pv = pev_ref[...].astype(BF16)
    g1 = _dot(av, wva[...])
    g2 = _dot(av, wvb[...])
    g2n = _dot(avn_ref[0], wvb[...])
    pev = _dot(pv, wva[...])[0:1] + _dot(pv, wvb[...])[1:2]
    hv = jax.nn.gelu(g1 + _shift_up(g2, g2n) + pev + bv_ref[...], approximate=True)
    v_out[0] = _dot(hv.astype(BF16), wv2[...]).astype(BF16)


def _compress(ak, av, pek2, wk, bk, wk2, pev2, wv, bv, wv2, k_norm, cosc, sinc):
    bg, nh, kk = ak.shape
    hid = bk.shape[1]
    tiles = nh // CMP_TILE
    cur = pl.BlockSpec((1, CMP_TILE, kk), lambda g, t: (g, t, 0))

    def nxt(rows):
        per = CMP_TILE // rows
        last = nh // rows - 1
        return pl.BlockSpec((1, rows, kk), lambda g, t: (g, jnp.minimum((t + 1) * per, last), 0))

    full = lambda shp: pl.BlockSpec(shp, lambda g, t: tuple(0 for _ in shp))
    w1 = full((kk, hid))
    out = pl.BlockSpec((1, CMP_TILE, HEAD_DIM), lambda g, t: (g, t, 0))
    tab = pl.BlockSpec((1, CMP_TILE, HEAD_DIM), lambda g, t: (g // N_KV, t, 0))
    oshape = jax.ShapeDtypeStruct((bg, nh, HEAD_DIM), BF16)
    return pl.pallas_call(
        _compress_kernel,
        grid=(bg, tiles),
        in_specs=[cur, nxt(8), cur, nxt(16),
                  full((8, kk)), w1, w1, w1, w1, full((1, hid)), full((hid, HEAD_DIM)),
                  full((hid, HEAD_DIM)),
                  full((8, kk)), w1, w1, full((1, hid)), full((hid, HEAD_DIM)),
                  full((3, HEAD_DIM)), tab, tab],
        out_specs=[out, out, out],
        out_shape=[oshape, oshape, oshape],
        compiler_params=_cparams(("arbitrary", "arbitrary")),
        name="compress",
    )(ak, ak, av, av, pek2, *wk, bk, *wk2, pev2, *wv, bv, wv2, k_norm, cosc, sinc)


def _count(mask):
    return jnp.sum(jnp.where(mask, 1.0, 0.0), axis=0, keepdims=True)


def _select_blocks(imp, t_pos):
    n_blocks = imp.shape[0]
    j = lax.broadcasted_iota(jnp.int32, (n_blocks, 1), 0)
    cur = jnp.right_shift(t_pos, SEL_BLOCK.bit_length() - 1)
    valid = j * SEL_BLOCK <= t_pos
    forced = (j == 0) | (j == cur) | (j == cur - 1)
    cand = valid & jnp.logical_not(forced)
    n_forced = 1.0 + jnp.where(cur >= 1, 1.0, 0.0) + jnp.where(cur >= 2, 1.0, 0.0)
    k = float(SEL_TOP_N) - n_forced

    imp_c = jnp.where(cand, imp, -1.0)

    def value_step(it, tb):
        trial = tb | jnp.left_shift(jnp.int32(1), 30 - it)
        keep = _count(imp_c >= pltpu.bitcast(trial, F32)) >= k
        return jnp.where(keep, trial, tb)

    tb = lax.fori_loop(0, 31, value_step, jnp.zeros(t_pos.shape, jnp.int32))
    thr = pltpu.bitcast(tb, F32)
    above = imp_c > thr
    need = k - _count(above)

    j_tied = jnp.where(imp_c == thr, j, 2 * n_blocks)

    def index_step(it, jb):
        trial = jb | jnp.left_shift(jnp.int32(1), (n_blocks.bit_length() - 1) - it)
        keep = _count(j_tied < trial) < need
        return jnp.where(keep, trial, jb)

    def index_search():
        return lax.fori_loop(0, n_blocks.bit_length(), index_step,
                             jnp.zeros(t_pos.shape, jnp.int32))

    all_tied_taken = jnp.min(need - _count(j_tied < 2 * n_blocks)) >= 0.0
    jb = lax.cond(all_tied_taken,
                  lambda: jnp.full(t_pos.shape, 2 * n_blocks - 1, jnp.int32), index_search)
    return jnp.where((forced & valid) | above | (j_tied <= jb), 1.0, 0.0)


def _attn_kernel(q_ref, kch_ref, kcl_ref, vc_ref, ksl_ref, vsl_ref, kw_ref, vw_ref, kmax_ref,
                 gate_ref, gn_ref, ovt_ref, o_ref, qa_ref, m_ref, l_ref, acc_ref, oslc_ref, s_ref):
    gi = pl.program_id(1)
    i = pl.program_id(2)
    tq = q_ref.shape[2]
    rows = GROUP_R * tq
    nc = kch_ref.shape[1]
    nsp = ovt_ref.shape[0]
    ck = SEL_CHUNK
    blocks_per_chunk = ck // SEL_BLOCK

    q = q_ref[0].reshape(rows, HEAD_DIM)
    q_hi, q_lo = _split(q)
    t_row = i * tq + jnp.bitwise_and(lax.broadcasted_iota(jnp.int32, (rows, 1), 0), tq - 1)

    k_hi = kch_ref[0]
    s = (_dot_nt(q_hi, kcl_ref[0])
         + _dot_nt(jnp.concatenate([q_hi, q_lo], axis=1), jnp.concatenate([k_hi, k_hi], axis=1)))
    n_end = lax.broadcasted_iota(jnp.int32, (1, nc), 1) * CMP_STRIDE + (CMP_BLOCK - 1)
    vis = n_end <= t_row
    s = jnp.where(vis, s, NEG_INF)
    p = jnp.where(vis, jnp.exp2(s - jnp.max(s, axis=-1, keepdims=True)), 0.0)
    l = jnp.sum(p, axis=-1, keepdims=True)
    some = l > 0.0
    pc = p * jnp.where(some, 1.0 / jnp.where(some, l, 1.0), 0.0)
    o_cmp = _dot(pc.astype(BF16), vc_ref[0])

    psum = pc[0:tq]
    for r in range(1, GROUP_R):
        psum = psum + pc[r * tq:(r + 1) * tq]
    ps_hi, ps_lo = _split(psum)
    ovt = ovt_ref[...]
    imp_t = _dot_nt(ovt, ps_hi) + _dot_nt(ovt, ps_lo)
    t_pos = i * tq + lax.broadcasted_iota(jnp.int32, (1, tq), 1)
    sel = jnp.transpose(_select_blocks(imp_t, t_pos))
    sel = jnp.concatenate([sel] * GROUP_R, axis=0) > 0.5

    def augmented_keys(c):
        k0 = pl.multiple_of(c * ck, ck)
        kblk = c * blocks_per_chunk + jnp.right_shift(
            lax.broadcasted_iota(jnp.int32, (ck, 1), 0), SEL_BLOCK.bit_length() - 1)
        lane = lax.broadcasted_iota(jnp.int32, (1, LANES), 1)
        onehot = jnp.where(jnp.bitwise_and(kblk, LANES - 1) == lane, 1.0, 0.0).astype(BF16)
        return k0, jnp.concatenate([ksl_ref[0, 0, pl.ds(k0, ck), :], onehot], axis=1)

    def scores(c, causal):
        k0, ka = augmented_keys(c)
        sc = _dot_nt(qa_ref[c // (LANES // blocks_per_chunk)], ka)
        if causal:
            kp = c * ck + lax.broadcasted_iota(jnp.int32, (1, ck), 1)
            sc = jnp.where(kp <= t_row, sc, NEG_INF)
        return k0, sc

    def set_lane_values(selected_value):
        lanev = jnp.where(sel, selected_value, NEG_INF).astype(BF16)
        for w in range(nsp // LANES):
            qa_ref[w] = jnp.concatenate([q_hi, lanev[:, w * LANES:(w + 1) * LANES]], axis=1)

    c_last = (i * tq) // ck

    def over_chunks(chunk):
        def past(c, carry):
            chunk(c, causal=False)
            return carry
        lax.fori_loop(0, c_last, past, 0)
        chunk(c_last, causal=True)

    qf = q_hi.astype(F32)
    bound = jnp.sqrt(jnp.sum(qf * qf, axis=-1, keepdims=True) * kmax_ref[0, 0][0:1, 0:1]) * 1.01
    safe = jnp.max(bound) <= SAFE_BOUND

    @pl.when(safe)
    def _():
        set_lane_values(SHIFT_MARGIN - bound)
        acc_ref[...] = jnp.zeros(acc_ref.shape, F32)
        ones = jnp.ones((ck, HEAD_DIM), BF16)

        def produce(c, slot):
            s_ref[slot] = scores(jnp.minimum(c, c_last), causal=True)[1]

        def consume(c, slot):
            k0 = pl.multiple_of(c * ck, ck)
            va = jnp.concatenate([vsl_ref[0, 0, pl.ds(k0, ck), :], ones], axis=1)
            acc_ref[...] += _dot(jnp.exp2(s_ref[slot]).astype(BF16), va)

        def pair(kk, carry):
            c = 2 * kk
            produce(c + 1, 1)
            consume(c, 0)
            produce(c + 2, 0)
            consume(c + 1, 1)
            return carry

        n_chunks = c_last + 1
        produce(0, 0)
        lax.fori_loop(0, n_chunks // 2, pair, 0)

        @pl.when(n_chunks % 2 == 1)
        def _():
            consume(c_last, 0)

        oslc_ref[...] = acc_ref[:, 0:HEAD_DIM] / acc_ref[:, HEAD_DIM:HEAD_DIM + 1]

    @pl.when(jnp.logical_not(safe))
    def _():
        set_lane_values(0.0)
        m_ref[...] = jnp.full(m_ref.shape, NEG_INF, F32)
        l_ref[...] = jnp.zeros(l_ref.shape, F32)
        acc_ref[...] = jnp.zeros(acc_ref.shape, F32)

        def chunk(c, causal):
            k0, sc = scores(c, causal)
            m_old = m_ref[...]
            m_new = jnp.maximum(m_old, jnp.max(sc, axis=-1, keepdims=True))
            alpha = jnp.exp2(m_old - m_new)
            pe = jnp.exp2(sc - m_new)
            l_ref[...] = alpha * l_ref[...] + jnp.sum(pe, axis=-1, keepdims=True)
            acc_ref[:, 0:HEAD_DIM] = alpha * acc_ref[:, 0:HEAD_DIM] + _dot(
                pe.astype(BF16), vsl_ref[0, 0, pl.ds(k0, ck), :])
            m_ref[...] = m_new

        over_chunks(chunk)
        oslc_ref[...] = acc_ref[:, 0:HEAD_DIM] / l_ref[...]

    o_slc = oslc_ref[...]

    wk = WINDOW + tq
    start = pl.multiple_of(jnp.maximum(i * tq - WINDOW, 0), tq)
    sw = _dot_nt(q_hi, kw_ref[0, 0, pl.ds(start, wk), :])
    kp = start + lax.broadcasted_iota(jnp.int32, (1, wk), 1)
    sw = jnp.where((kp <= t_row) & (kp > t_row - WINDOW), sw, NEG_INF)
    pw = jnp.exp2(sw - jnp.max(sw, axis=-1, keepdims=True))
    o_win = _dot(pw.astype(BF16), vw_ref[0, 0, pl.ds(start, wk), :]) / jnp.sum(pw, axis=-1, keepdims=True)

    gates = gate_ref[0]
    gn = gn_ref[...]
    n_gate = 3 * GROUP_R
    for r in range(GROUP_R):
        sl = slice(r * tq, (r + 1) * tq)

        def gate(branch, r=r):
            col = 3 * r + branch
            return jnp.where(gi == 0, gates[:, col:col + 1], gates[:, n_gate + col:n_gate + col + 1])

        o = gate(0) * o_cmp[sl] + gate(1) * o_slc[sl] + gate(2) * o_win[sl]
        hs = slice(r * HEAD_DIM, (r + 1) * HEAD_DIM)
        o_ref[0, :, hs] = _rms(o, gn[:, hs]).astype(o_ref.dtype)


def _attn(q_r, kc_hi, kc_lo, v_cmp, ksl, vsl, kw, vw, kmax, gates, gain, ovt):
    batch, _, seq, _ = q_r.shape
    nc = kc_hi.shape[1]
    nsp = ovt.shape[0]
    tq = Q_TILE
    rows = GROUP_R * tq
    gw = GROUP_R * HEAD_DIM
    cmp_spec = pl.BlockSpec((1, nc, HEAD_DIM), lambda b, g, i: (b * N_KV + g, 0, 0))
    kv_spec = pl.BlockSpec((1, 1, seq, HEAD_DIM), lambda b, g, i: (b, g, 0, 0))
    return pl.pallas_call(
        _attn_kernel,
        grid=(batch, N_KV, seq // tq),
        in_specs=[pl.BlockSpec((1, GROUP_R, tq, HEAD_DIM), lambda b, g, i: (b, g, i, 0)),
                  cmp_spec, cmp_spec, cmp_spec,
                  kv_spec, kv_spec, kv_spec, kv_spec,
                  pl.BlockSpec((1, 1, 8, LANES), lambda b, g, i: (b, g, 0, 0)),
                  pl.BlockSpec((1, tq, LANES), lambda b, g, i: (b, i, 0)),
                  pl.BlockSpec((1, gw), lambda b, g, i: (0, g)),
                  pl.BlockSpec((nsp, nc), lambda b, g, i: (0, 0))],
        out_specs=pl.BlockSpec((1, tq, gw), lambda b, g, i: (b, i, g)),
        out_shape=jax.ShapeDtypeStruct((batch, seq, N_KV * gw), BF16),
        scratch_shapes=[pltpu.VMEM((nsp // LANES, rows, 2 * HEAD_DIM), BF16),
                        pltpu.VMEM((rows, 1), F32), pltpu.VMEM((rows, 1), F32),
                        pltpu.VMEM((rows, 2 * HEAD_DIM), F32),
                        pltpu.VMEM((rows, HEAD_DIM), F32),
                        pltpu.VMEM((2, rows, SEL_CHUNK), F32)],
        compiler_params=_cparams(("arbitrary", "arbitrary", "arbitrary")),
        name="attn",
    )(q_r, kc_hi, kc_lo, v_cmp, ksl, vsl, kw, vw, kmax, gates, gain, ovt)


def _outproj_kernel(yc_ref, ya_ref, x_ref, w_ref, ga_ref, g_ref, sh_ref, sc_ref, x1_ref, h2_ref):
    cw = yc_ref.shape[1]
    acc = _dot(yc_ref[...], w_ref[0:cw, :]) + _dot(ya_ref[...], w_ref[cw:, :])
    x1 = x_ref[...] + ga_ref[0] * acc
    x1_ref[...] = x1
    h2_ref[...] = (_rms(x1, g_ref[...]) * (1.0 + sc_ref[0]) + sh_ref[0]).astype(h2_ref.dtype)


def _outproj(y_conv, y_attn, x2, w_out, mod3, gain, seq):
    t, d = x2.shape
    tiles_per_seq = seq // ROW_TILE
    mod = lambda k: pl.BlockSpec((1, 1, d), lambda i: (i // tiles_per_seq, 0, k))
    row = lambda w: pl.BlockSpec((ROW_TILE, w), lambda i: (i, 0))
    return pl.pallas_call(
        _outproj_kernel,
        grid=(t // ROW_TILE,),
        in_specs=[row(y_conv.shape[1]), row(y_attn.shape[1]), row(d),
                  pl.BlockSpec(w_out.shape, lambda i: (0, 0)),
                  mod(2), pl.BlockSpec((1, d), lambda i: (0, 0)), mod(3), mod(4)],
        out_specs=[row(d), row(d)],
        out_shape=[jax.ShapeDtypeStruct((t, d), F32), jax.ShapeDtypeStruct((t, d), BF16)],
        compiler_params=_cparams(("arbitrary",)),
        name="outproj",
    )(y_conv, y_attn, x2, w_out, mod3, gain, mod3, mod3)


def _ffn_kernel(h_ref, w1_ref, w3_ref, w2_ref, x1_ref, gf_ref, o_ref, acc_ref):
    f = pl.program_id(1)
    h = h_ref[...]
    act = (jax.nn.silu(_dot(h, w1_ref[...])) * _dot(h, w3_ref[...])).astype(BF16)
    part = _dot(act, w2_ref[...])

    @pl.when(f == 0)
    def _():
        acc_ref[...] = part

    @pl.when(f > 0)
    def _():
        acc_ref[...] += part

    @pl.when(f == pl.num_programs(1) - 1)
    def _():
        o_ref[...] = x1_ref[...] + gf_ref[0] * acc_ref[...]


def _ffn(h2, w1, w3, w2, x1, mod3, seq):
    t, d = x1.shape
    dff = w1.shape[1]
    tiles_per_seq = seq // ROW_TILE
    return pl.pallas_call(
        _ffn_kernel,
        grid=(t // ROW_TILE, dff // FFN_TF),
        in_specs=[pl.BlockSpec((ROW_TILE, d), lambda i, f: (i, 0)),
                  pl.BlockSpec((d, FFN_TF), lambda i, f: (0, f)),
                  pl.BlockSpec((d, FFN_TF), lambda i, f: (0, f)),
                  pl.BlockSpec((FFN_TF, d), lambda i, f: (f, 0)),
                  pl.BlockSpec((ROW_TILE, d), lambda i, f: (i, 0)),
                  pl.BlockSpec((1, 1, d), lambda i, f: (i // tiles_per_seq, 0, 5))],
        out_specs=pl.BlockSpec((ROW_TILE, d), lambda i, f: (i, 0)),
        out_shape=jax.ShapeDtypeStruct((t, d), F32),
        scratch_shapes=[pltpu.VMEM((ROW_TILE, d), F32)],
        compiler_params=_cparams(("arbitrary", "arbitrary")),
        name="ffn",
    )(h2, w1, w3, w2, x1, mod3)


def _overlap_matrix(n_cmp_pad, n_sel, n_sel_pad):
    cmp_start = np.arange(n_cmp_pad) * CMP_STRIDE
    cmp_end = cmp_start + CMP_BLOCK - 1
    sel_start = np.arange(n_sel) * SEL_BLOCK
    ov = ((cmp_start[None, :] < sel_start[:, None] + SEL_BLOCK)
          & (cmp_end[None, :] >= sel_start[:, None])).astype(np.float32)
    return np.pad(ov, ((0, n_sel_pad - n_sel), (0, 0)))


def _w1_halves(w1):
    l, dk, hid = w1.shape
    half = l // 2
    return w1[:half].reshape(half * dk, hid), w1[half:].reshape(half * dk, hid)


def _pe_rows(pe):
    l, dk = pe.shape
    half = l // 2
    rows = jnp.stack([pe[:half].reshape(half * dk), pe[half:].reshape(half * dk)])
    return jnp.pad(rows, ((0, 6), (0, 0)))


def kernel(x, c, positions, ada_w, ada_b, norm_mix, norm_ffn, w_in, conv_w, cmp_pe_k, cmp_k_w1, cmp_k_b1, cmp_k_w2, cmp_pe_v, cmp_v_w1, cmp_v_b1, cmp_v_w2, q_norm, k_norm, out_norm_conv, out_norm_attn, w_out, ffn_w1, ffn_w3, ffn_w2):
    batch, seq, d = x.shape
    t = batch * seq
    depth = ada_w.shape[0]
    cw = conv_w.shape[2]
    aw = N_HEADS * HEAD_DIM
    kvw = N_KV * HEAD_DIM
    n_gate = 3 * N_HEADS
    assert seq % max(ROW_TILE, SEL_CHUNK) == 0 and seq >= WINDOW + Q_TILE
    assert seq // SEL_BLOCK >= SEL_TOP_N and CMP_BLOCK == 2 * CMP_STRIDE
    nh = seq // CMP_STRIDE
    n_sel = seq // SEL_BLOCK
    n_sel_pad = -(-n_sel // LANES) * LANES
    n_sel_pad = 1 << (n_sel_pad - 1).bit_length()
    assert nh % CMP_TILE == 0

    inv = 1.0 / (ROPE_THETA ** (jnp.arange(0, HEAD_DIM, 2, dtype=F32) / HEAD_DIM))
    inv_full = jnp.concatenate([inv, inv])[None, :]
    sign = jnp.concatenate([-jnp.ones(HEAD_DIM // 2, F32), jnp.ones(HEAD_DIM // 2, F32)])[None, :]
    cosf, sinf = _rope_tables(positions.reshape(t, 1), inv_full, sign)
    at_end = lambda tab: jnp.pad(
        tab.reshape(batch, nh, CMP_STRIDE, HEAD_DIM)[:, 1:, CMP_STRIDE - 1], ((0, 0), (0, 1), (0, 0)))
    cosc, sinc = at_end(cosf), at_end(sinf)
    ovt = jnp.asarray(_overlap_matrix(nh, n_sel, n_sel_pad), BF16)

    c8 = jnp.pad(c, ((0, 8 - batch), (0, 0)))
    x2 = x.reshape(t, d)
    for l in range(depth):
        mod = _ada(c8, ada_w[l], ada_b[l][None, :])
        mod3 = mod[:batch].reshape(batch, 1, 6 * d)

        w = w_in[l]
        o_q = 3 * cw
        o_kc = o_q + aw
        o_rest = o_kc + kvw
        o_gl = o_rest + 5 * kvw
        main_w = o_q + 5 * kvw + n_gate
        main_pad = -(-main_w // INPROJ_TN) * INPROJ_TN
        w_main = jnp.concatenate(
            [w[:, :o_q], w[:, o_rest:o_gl], w[:, o_gl:o_gl + n_gate],
             jnp.zeros((d, main_pad - main_w), F32)], axis=1).astype(BF16)
        w_hi, w_lo = _split(w[:, o_q:o_rest])
        pm, pp = _inproj(x2, mod3, norm_mix[l][None, :], w_main, w_hi, w_lo, seq)

        y_conv = _conv(pm, conv_w[l], out_norm_conv[l][None, :], seq, cw)

        q_r, kc_t, vc_t, ksl, vsl, kw, vw, gates, kmax = _prep(
            pm, pp, cosf, sinf, q_norm[l][None, :], k_norm[l], batch, seq, o_q, o_q + 5 * kvw)

        wka, wkb = _w1_halves(cmp_k_w1[l])
        wva, wvb = _w1_halves(cmp_v_w1[l])
        kc_hi, kc_lo, v_cmp = _compress(
            kc_t.reshape(batch * N_KV, nh, CMP_STRIDE * HEAD_DIM),
            vc_t.reshape(batch * N_KV, nh, CMP_STRIDE * HEAD_DIM),
            _pe_rows(cmp_pe_k[l]), (*_split(wka), *_split(wkb)), cmp_k_b1[l][None, :],
            _split(cmp_k_w2[l]),
            _pe_rows(cmp_pe_v[l]), (wva.astype(BF16), wvb.astype(BF16)), cmp_v_b1[l][None, :],
            cmp_v_w2[l].astype(BF16), k_norm[l], cosc, sinc)

        y_attn = _attn(q_r, kc_hi, kc_lo, v_cmp, ksl, vsl, kw, vw, kmax, gates,
                       out_norm_attn[l][None, :], ovt)

        x1, h2 = _outproj(y_conv, y_attn.reshape(t, aw), x2, w_out[l].astype(BF16), mod3,
                          norm_ffn[l][None, :], seq)
        x2 = _ffn(h2, ffn_w1[l].astype(BF16), ffn_w3[l].astype(BF16), ffn_w2[l].astype(BF16),
                  x1, mod3, seq)
    return x2.reshape(batch, seq, d)
```

```python
import functools

import numpy as np
import jax
import jax.numpy as jnp
from jax import lax
from jax.experimental import pallas as pl
from jax.experimental.pallas import tpu as pltpu

F32 = jnp.float32
BF16 = jnp.bfloat16

CONV_GROUP = 128
CONV_KSIZE = 3
HEAD_DIM = 128
N_KV = 2
GROUP_R = 4
N_HEADS = N_KV * GROUP_R
CMP_BLOCK = 32
CMP_STRIDE = 16
SEL_BLOCK = 64
SEL_TOP_N = 16
WINDOW = 512
ROPE_THETA = 10000.0
EPS = 1e-6
NEG_INF = -1e30
LOG2E = 1.4426950408889634
SHIFT_MARGIN = 57.0
SAFE_BOUND = 86.0

LANES = 128
VMEM_LIMIT = 56 * 1024 * 1024

ROW_TILE = 512
Q_TILE = 256
SEL_CHUNK = 512
CMP_TILE = 256
ADA_TN = 1024
INPROJ_TN = 768
FFN_TF = 512
FFN_TM = 1024
FFN_SUB = 2


def _cparams(sem):
    return pltpu.CompilerParams(dimension_semantics=sem, vmem_limit_bytes=VMEM_LIMIT)


def _split(a):
    hi = a.astype(BF16)
    lo = (a - hi.astype(F32)).astype(BF16)
    return hi, lo


def _dot(a, b):
    return jnp.dot(a, b, preferred_element_type=F32)


def _dot_nt(a, b):
    return lax.dot_general(a, b, (((1,), (1,)), ((), ())), preferred_element_type=F32)


def _dot3(a_hi, a_lo, b_hi, b_lo, dot=_dot):
    return (dot(a_hi, b_lo) + dot(a_lo, b_hi)) + dot(a_hi, b_hi)


def _rms(x, gain):
    return x * lax.rsqrt(jnp.mean(x * x, axis=-1, keepdims=True) + EPS) * gain


def _rope(y, cosf, sinf):
    return y * cosf + pltpu.roll(y, HEAD_DIM // 2, 1) * sinf


def _ada_kernel(c_ref, w_ref, b_ref, o_ref):
    s_hi, s_lo = _split(jax.nn.silu(c_ref[...]))
    w_hi, w_lo = _split(w_ref[...])
    o_ref[...] = _dot3(s_hi, s_lo, w_hi, w_lo) + b_ref[...]


def _ada(c8, w, b):
    d, n = w.shape
    return pl.pallas_call(
        _ada_kernel,
        grid=(n // ADA_TN,),
        in_specs=[pl.BlockSpec((8, d), lambda j: (0, 0)),
                  pl.BlockSpec((d, ADA_TN), lambda j: (0, j)),
                  pl.BlockSpec((1, ADA_TN), lambda j: (0, j))],
        out_specs=pl.BlockSpec((8, ADA_TN), lambda j: (0, j)),
        out_shape=jax.ShapeDtypeStruct((8, n), F32),
        compiler_params=_cparams(("arbitrary",)),
        name="ada",
    )(c8, w, b)


def _inproj_kernel(x_ref, sh_ref, sc_ref, g_ref, wm_ref, whi_ref, wlo_ref,
                   om_ref, op_ref, hhi_ref, hlo_ref, *, n_main):
    j = pl.program_id(1)

    @pl.when(j == 0)
    def _():
        h = _rms(x_ref[...], g_ref[...]) * (1.0 + sc_ref[0]) + sh_ref[0]
        hi, lo = _split(h)
        hhi_ref[...] = hi
        hlo_ref[...] = lo

    @pl.when(j < n_main)
    def _():
        om_ref[...] = _dot(hhi_ref[...], wm_ref[...]).astype(om_ref.dtype)

    @pl.when(j == n_main)
    def _():
        op_ref[...] = _dot3(hhi_ref[...], hlo_ref[...], whi_ref[...], wlo_ref[...])


def _inproj(x2, mod3, gain, w_main, w_hi, w_lo, seq):
    t, d = x2.shape
    n_main = w_main.shape[1] // INPROJ_TN
    n_prec = w_hi.shape[1]
    tiles_per_seq = seq // ROW_TILE
    return pl.pallas_call(
        functools.partial(_inproj_kernel, n_main=n_main),
        grid=(t // ROW_TILE, n_main + 1),
        in_specs=[
            pl.BlockSpec((ROW_TILE, d), lambda i, j: (i, 0)),
            pl.BlockSpec((1, 1, d), lambda i, j: (i // tiles_per_seq, 0, 0)),
            pl.BlockSpec((1, 1, d), lambda i, j: (i // tiles_per_seq, 0, 1)),
            pl.BlockSpec((1, d), lambda i, j: (0, 0)),
            pl.BlockSpec((d, INPROJ_TN), lambda i, j: (0, jnp.minimum(j, n_main - 1))),
            pl.BlockSpec((d, n_prec), lambda i, j: (0, 0)),
            pl.BlockSpec((d, n_prec), lambda i, j: (0, 0)),
        ],
        out_specs=[
            pl.BlockSpec((ROW_TILE, INPROJ_TN), lambda i, j: (i, jnp.minimum(j, n_main - 1))),
            pl.BlockSpec((ROW_TILE, n_prec), lambda i, j: (i, 0)),
        ],
        out_shape=[jax.ShapeDtypeStruct((t, w_main.shape[1]), BF16),
                   jax.ShapeDtypeStruct((t, n_prec), F32)],
        scratch_shapes=[pltpu.VMEM((ROW_TILE, d), BF16), pltpu.VMEM((ROW_TILE, d), BF16)],
        compiler_params=_cparams(("arbitrary", "arbitrary")),
        name="inproj",
    )(x2, mod3, mod3, gain, w_main, w_hi, w_lo)


HALO = 16


def _conv_kernel(cb_ref, cc_ref, ch_ref, ccp_ref, chp_ref, w_ref, g_ref, o_ref, *, tiles_per_seq):
    i = pl.program_id(0)
    u = cc_ref[...].astype(F32) * ch_ref[...].astype(F32)
    up = ccp_ref[...].astype(F32) * chp_ref[...].astype(F32)
    up = jnp.where(i % tiles_per_seq == 0, 0.0, up)
    rows = lax.broadcasted_iota(jnp.int32, u.shape, 0)
    u1 = jnp.where(rows == 0, up[HALO - 1:HALO], pltpu.roll(u, 1, 0))
    u2 = jnp.where(rows == 0, up[HALO - 2:HALO - 1],
                   jnp.where(rows == 1, up[HALO - 1:HALO], pltpu.roll(u, 2, 0)))
    w = w_ref[...]
    y = cb_ref[...].astype(F32) * (w[2:3] * u + w[1:2] * u1 + w[0:1] * u2)
    gain = g_ref[...]
    for g in range(y.shape[1] // CONV_GROUP):
        sl = slice(g * CONV_GROUP, (g + 1) * CONV_GROUP)
        o_ref[:, sl] = _rms(y[:, sl], gain[:, sl]).astype(o_ref.dtype)


def _conv(pm, conv_w, gain, seq, cw):
    t = pm.shape[0]
    tiles_per_seq = seq // ROW_TILE
    hb = ROW_TILE // HALO

    def prev(col):
        return pl.BlockSpec((HALO, cw), lambda i: (jnp.maximum(i * hb - 1, 0), col))

    return pl.pallas_call(
        functools.partial(_conv_kernel, tiles_per_seq=tiles_per_seq),
        grid=(t // ROW_TILE,),
        in_specs=[pl.BlockSpec((ROW_TILE, cw), lambda i: (i, 0)),
                  pl.BlockSpec((ROW_TILE, cw), lambda i: (i, 1)),
                  pl.BlockSpec((ROW_TILE, cw), lambda i: (i, 2)),
                  prev(1), prev(2),
                  pl.BlockSpec((CONV_KSIZE, cw), lambda i: (0, 0)),
                  pl.BlockSpec((1, cw), lambda i: (0, 0))],
        out_specs=pl.BlockSpec((ROW_TILE, cw), lambda i: (i, 0)),
        out_shape=jax.ShapeDtypeStruct((t, cw), BF16),
        compiler_params=_cparams(("arbitrary",)),
        name="conv",
    )(pm, pm, pm, pm, pm, conv_w, gain)


def _rope_kernel(pos_ref, inv_ref, sign_ref, cos_ref, sin_ref):
    ang = pos_ref[...].astype(F32) * inv_ref[...]
    cos_ref[...] = jnp.cos(ang)
    sin_ref[...] = jnp.sin(ang) * sign_ref[...]


def _rope_tables(pos2, inv_full, sign):
    t = pos2.shape[0]
    ts = 2048
    return pl.pallas_call(
        _rope_kernel,
        grid=(t // ts,),
        in_specs=[pl.BlockSpec((ts, 1), lambda i: (i, 0)),
                  pl.BlockSpec((1, HEAD_DIM), lambda i: (0, 0)),
                  pl.BlockSpec((1, HEAD_DIM), lambda i: (0, 0))],
        out_specs=[pl.BlockSpec((ts, HEAD_DIM), lambda i: (i, 0))] * 2,
        out_shape=[jax.ShapeDtypeStruct((t, HEAD_DIM), F32)] * 2,
        compiler_params=_cparams(("arbitrary",)),
        name="rope",
    )(pos2, inv_full, sign)


def _prep_kernel(pp_ref, vc_ref, ksl_ref, vsl_ref, kw_ref, vw_ref, gl_ref, cos_ref, sin_ref,
                 qn_ref, kn_ref,
                 q_out, kc_out, vc_out, ksl_out, vsl_out, kw_out, vw_out, gate_out, kmax_out):
    cosf = cos_ref[...]
    sinf = sin_ref[...]
    scale = HEAD_DIM ** -0.5 * LOG2E
    qn = qn_ref[...]
    kn = kn_ref[...]
    for h in range(N_HEADS):
        qh = pp_ref[:, h * HEAD_DIM:(h + 1) * HEAD_DIM]
        q_out[0, h] = _rope(_rms(qh, qn), cosf, sinf) * scale
    qw = N_HEADS * HEAD_DIM
    for g in range(N_KV):
        sl = slice(g * HEAD_DIM, (g + 1) * HEAD_DIM)
        kc_out[0, g] = pp_ref[:, qw + g * HEAD_DIM:qw + (g + 1) * HEAD_DIM]
        vc_out[0, g] = vc_ref[:, sl]
        vsl_out[0, g] = vsl_ref[:, sl]
        vw_out[0, g] = vw_ref[:, sl]
        ksl = _rope(_rms(ksl_ref[:, sl].astype(F32), kn[1:2]), cosf, sinf).astype(BF16)
        ksl_out[0, g] = ksl
        kw_out[0, g] = _rope(_rms(kw_ref[:, sl].astype(F32), kn[2:3]), cosf, sinf).astype(BF16)
        kf = ksl.astype(F32)
        n2 = jnp.full(kmax_out.shape[2:], jnp.max(jnp.sum(kf * kf, axis=-1, keepdims=True)), F32)

        @pl.when(pl.program_id(1) == 0)
        def _(g=g, n2=n2):
            kmax_out[0, g] = n2

        @pl.when(pl.program_id(1) > 0)
        def _(g=g, n2=n2):
            kmax_out[0, g] = jnp.maximum(kmax_out[0, g], n2)
    gate_out[0] = jax.nn.sigmoid(gl_ref[...].astype(F32))


def _prep(pm, pp, cosf, sinf, q_norm, k_norm, batch, seq, kv_col0, gl_col):
    ts = ROW_TILE
    tiles = seq // ts
    kvw = N_KV * HEAD_DIM
    kvb = kv_col0 // kvw

    def rows(width, col):
        return pl.BlockSpec((ts, width), lambda b, s: (b * tiles + s, col))

    def heads(n):
        return pl.BlockSpec((1, n, ts, HEAD_DIM), lambda b, s: (b, 0, s, 0))

    kv_shape = jax.ShapeDtypeStruct((batch, N_KV, seq, HEAD_DIM), BF16)
    return pl.pallas_call(
        _prep_kernel,
        grid=(batch, tiles),
        in_specs=[rows(pp.shape[1], 0),
                  rows(kvw, kvb), rows(kvw, kvb + 1), rows(kvw, kvb + 2), rows(kvw, kvb + 3),
                  rows(kvw, kvb + 4),
                  rows(LANES, gl_col // LANES),
                  rows(HEAD_DIM, 0), rows(HEAD_DIM, 0),
                  pl.BlockSpec((1, HEAD_DIM), lambda b, s: (0, 0)),
                  pl.BlockSpec((3, HEAD_DIM), lambda b, s: (0, 0))],
        out_specs=[heads(N_HEADS), heads(N_KV), heads(N_KV), heads(N_KV), heads(N_KV), heads(N_KV),
                   heads(N_KV),
                   pl.BlockSpec((1, ts, LANES), lambda b, s: (b, s, 0)),
                   pl.BlockSpec((1, N_KV, 8, LANES), lambda b, s: (b, 0, 0, 0))],
        out_shape=[jax.ShapeDtypeStruct((batch, N_HEADS, seq, HEAD_DIM), F32),
                   jax.ShapeDtypeStruct((batch, N_KV, seq, HEAD_DIM), F32),
                   kv_shape, kv_shape, kv_shape, kv_shape, kv_shape,
                   jax.ShapeDtypeStruct((batch, seq, LANES), F32),
                   jax.ShapeDtypeStruct((batch, N_KV, 8, LANES), F32)],
        compiler_params=_cparams(("arbitrary", "arbitrary")),
        name="prep",
    )(pp, pm, pm, pm, pm, pm, pm, cosf, sinf, q_norm, k_norm)


def _shift_up(h2, h2_next):
    n = h2.shape[0]
    rows = lax.broadcasted_iota(jnp.int32, h2.shape, 0)
    return jnp.where(rows == n - 1, h2_next[0:1], pltpu.roll(h2, n - 1, 0))


def _compress_kernel(ak_ref, akn_ref, av_ref, avn_ref,
                     pek_ref, wka_hi, wka_lo, wkb_hi, wkb_lo, bk_ref, wk2_hi, wk2_lo,
                     pev_ref, wva, wvb, bv_ref, wv2,
                     kn_ref, cos_ref, sin_ref,
                     kh_out, kl_out, v_out):
    a_hi, a_lo = _split(ak_ref[0])
    n_hi, n_lo = _split(akn_ref[0])
    p_hi, p_lo = _split(pek_ref[...])
    h1 = _dot3(a_hi, a_lo, wka_hi[...], wka_lo[...])
    h2 = _dot3(a_hi, a_lo, wkb_hi[...], wkb_lo[...])
    h2n = _dot3(n_hi, n_lo, wkb_hi[...], wkb_lo[...])
    pe = (_dot3(p_hi, p_lo, wka_hi[...], wka_lo[...])[0:1]
          + _dot3(p_hi, p_lo, wkb_hi[...], wkb_lo[...])[1:2])
    hid = jax.nn.gelu(h1 + _shift_up(h2, h2n) + pe + bk_ref[...], approximate=True)
    hid_hi, hid_lo = _split(hid)
    kc = _dot3(hid_hi, hid_lo, wk2_hi[...], wk2_lo[...])
    kc = _rope(_rms(kc, kn_ref[0:1]), cos_ref[0], sin_ref[0])
    k_hi, k_lo = _split(kc)
    kh_out[0] = k_hi
    kl_out[0] = k_lo

    av = av_ref[0]
    pv = pev_ref[...].astype(BF16)
    g1 = _dot(av, wva[...])
    g2 = _dot(av, wvb[...])
    g2n = _dot(avn_ref[0], wvb[...])
    pev = _dot(pv, wva[...])[0:1] + _dot(pv, wvb[...])[1:2]
    hv = jax.nn.gelu(g1 + _shift_up(g2, g2n) + pev + bv_ref[...], approximate=True)
    v_out[0] = _dot(hv.astype(BF16), wv2[...]).astype(BF16)


def _compress(ak, av, pek2, wk, bk, wk2, pev2, wv, bv, wv2, k_norm, cosc, sinc):
    bg, nh, kk = ak.shape
    hid = bk.shape[1]
    tiles = nh // CMP_TILE
    cur = pl.BlockSpec((1, CMP_TILE, kk), lambda g, t: (g, t, 0))

    def nxt(rows):
        per = CMP_TILE // rows
        last = nh // rows - 1
        return pl.BlockSpec((1, rows, kk), lambda g, t: (g, jnp.minimum((t + 1) * per, last), 0))

    full = lambda shp: pl.BlockSpec(shp, lambda g, t: tuple(0 for _ in shp))
    w1 = full((kk, hid))
    out = pl.BlockSpec((1, CMP_TILE, HEAD_DIM), lambda g, t: (g, t, 0))
    tab = pl.BlockSpec((1, CMP_TILE, HEAD_DIM), lambda g, t: (g // N_KV, t, 0))
    oshape = jax.ShapeDtypeStruct((bg, nh, HEAD_DIM), BF16)
    return pl.pallas_call(
        _compress_kernel,
        grid=(bg, tiles),
        in_specs=[cur, nxt(8), cur, nxt(16),
                  full((8, kk)), w1, w1, w1, w1, full((1, hid)), full((hid, HEAD_DIM)),
                  full((hid, HEAD_DIM)),
                  full((8, kk)), w1, w1, full((1, hid)), full((hid, HEAD_DIM)),
                  full((3, HEAD_DIM)), tab, tab],
        out_specs=[out, out, out],
        out_shape=[oshape, oshape, oshape],
        compiler_params=_cparams(("arbitrary", "arbitrary")),
        name="compress",
    )(ak, ak, av, av, pek2, *wk, bk, *wk2, pev2, *wv, bv, wv2, k_norm, cosc, sinc)


def _count(mask):
    return jnp.sum(jnp.where(mask, 1.0, 0.0), axis=0, keepdims=True)


def _select_blocks(imp, t_pos):
    n_blocks = imp.shape[0]
    j = lax.broadcasted_iota(jnp.int32, (n_blocks, 1), 0)
    cur = jnp.right_shift(t_pos, SEL_BLOCK.bit_length() - 1)
    valid = j * SEL_BLOCK <= t_pos
    forced = (j == 0) | (j == cur) | (j == cur - 1)
    cand = valid & jnp.logical_not(forced)
    n_forced = 1.0 + jnp.where(cur >= 1, 1.0, 0.0) + jnp.where(cur >= 2, 1.0, 0.0)
    k = float(SEL_TOP_N) - n_forced

    imp_c = jnp.where(cand, imp, -1.0)

    def value_step(it, tb):
        trial = tb | jnp.left_shift(jnp.int32(1), 30 - it)
        keep = _count(imp_c >= pltpu.bitcast(trial, F32)) >= k
        return jnp.where(keep, trial, tb)

    tb = lax.fori_loop(0, 31, value_step, jnp.zeros(t_pos.shape, jnp.int32))
    thr = pltpu.bitcast(tb, F32)
    above = imp_c > thr
    need = k - _count(above)

    j_tied = jnp.where(imp_c == thr, j, 2 * n_blocks)

    def index_step(it, jb):
        trial = jb | jnp.left_shift(jnp.int32(1), (n_blocks.bit_length() - 1) - it)
        keep = _count(j_tied < trial) < need
        return jnp.where(keep, trial, jb)

    def index_search():
        return lax.fori_loop(0, n_blocks.bit_length(), index_step,
                             jnp.zeros(t_pos.shape, jnp.int32))

    all_tied_taken = jnp.min(need - _count(j_tied < 2 * n_blocks)) >= 0.0
    jb = lax.cond(all_tied_taken,
                  lambda: jnp.full(t_pos.shape, 2 * n_blocks - 1, jnp.int32), index_search)
    return jnp.where((forced & valid) | above | (j_tied <= jb), 1.0, 0.0)


def _attn_kernel(q_ref, kch_ref, kcl_ref, vc_ref, ksl_ref, vsl_ref, kw_ref, vw_ref, kmax_ref,
                 gate_ref, gn_ref, ovt_ref, o_ref, qa_ref, m_ref, l_ref, acc_ref, oslc_ref, s_ref,
                 ocmp_ref, impt_ref):
    gi = pl.program_id(1)
    i = pl.program_id(2)
    tq = q_ref.shape[2]
    rows = GROUP_R * tq
    nc = kch_ref.shape[1]
    nsp = ovt_ref.shape[0]
    ck = SEL_CHUNK
    blocks_per_chunk = ck // SEL_BLOCK

    q = q_ref[0].reshape(rows, HEAD_DIM)
    q_hi, q_lo = _split(q)
    t_row = i * tq + jnp.bitwise_and(lax.broadcasted_iota(jnp.int32, (rows, 1), 0), tq - 1)

    q_hl = jnp.concatenate([q_hi, q_lo], axis=1)

    def compressed(ncols):
        k_hi = kch_ref[0, 0:ncols, :]
        s = _dot_nt(q_hi, kcl_ref[0, 0:ncols, :]) + _dot_nt(q_hl, jnp.concatenate([k_hi, k_hi], axis=1))
        n_end = lax.broadcasted_iota(jnp.int32, (1, ncols), 1) * CMP_STRIDE + (CMP_BLOCK - 1)
        s = jnp.where(n_end <= t_row, s, NEG_INF)
        m = jnp.max(s, axis=-1, keepdims=True)
        p = jnp.exp2(s - m)
        l = jnp.sum(p, axis=-1, keepdims=True)
        pc = p * jnp.where(m > 0.5 * NEG_INF, 1.0 / l, 0.0)
        ocmp_ref[...] = _dot(pc.astype(BF16), vc_ref[0, 0:ncols, :])
        psum = pc[0:tq]
        for r in range(1, GROUP_R):
            psum = psum + pc[r * tq:(r + 1) * tq]
        ps_hi, ps_lo = _split(psum)
        ovt = ovt_ref[:, 0:ncols]
        impt_ref[...] = _dot_nt(ovt, ps_hi) + _dot_nt(ovt, ps_lo)

    first_half = 2 * i < pl.num_programs(2)

    @pl.when(first_half)
    def _():
        compressed(nc // 2)

    @pl.when(jnp.logical_not(first_half))
    def _():
        compressed(nc)

    o_cmp = ocmp_ref[...]

    t_pos = i * tq + lax.broadcasted_iota(jnp.int32, (1, tq), 1)
    sel = jnp.transpose(_select_blocks(impt_ref[...], t_pos))
    sel = jnp.concatenate([sel] * GROUP_R, axis=0) > 0.5

    def augmented_keys(c):
        k0 = pl.multiple_of(c * ck, ck)
        kblk = c * blocks_per_chunk + jnp.right_shift(
            lax.broadcasted_iota(jnp.int32, (ck, 1), 0), SEL_BLOCK.bit_length() - 1)
        lane = lax.broadcasted_iota(jnp.int32, (1, LANES), 1)
        onehot = jnp.where(jnp.bitwise_and(kblk, LANES - 1) == lane, 1.0, 0.0).astype(BF16)
        return k0, jnp.concatenate([ksl_ref[0, 0, pl.ds(k0, ck), :], onehot], axis=1)

    def scores(c, causal):
        k0, ka = augmented_keys(c)
        sc = _dot_nt(qa_ref[c // (LANES // blocks_per_chunk)], ka)
        if causal:
            kp = c * ck + lax.broadcasted_iota(jnp.int32, (1, ck), 1)
            sc = jnp.where(kp <= t_row, sc, NEG_INF)
        return k0, sc

    def set_lane_values(selected_value):
        lanev = jnp.where(sel, selected_value, NEG_INF).astype(BF16)
        for w in range(nsp // LANES):
            qa_ref[w] = jnp.concatenate([q_hi, lanev[:, w * LANES:(w + 1) * LANES]], axis=1)

    c_last = (i * tq) // ck

    def over_chunks(chunk):
        def past(c, carry):
            chunk(c, causal=False)
            return carry
        lax.fori_loop(0, c_last, past, 0)
        chunk(c_last, causal=True)

    qf = q_hi.astype(F32)
    bound = jnp.sqrt(jnp.sum(qf * qf, axis=-1, keepdims=True) * kmax_ref[0, 0][0:1, 0:1]) * 1.01
    safe = jnp.max(bound) <= SAFE_BOUND

    @pl.when(safe)
    def _():
        set_lane_values(SHIFT_MARGIN - bound)
        acc_ref[...] = jnp.zeros(acc_ref.shape, F32)
        ones = jnp.ones((ck, HEAD_DIM), BF16)

        def produce(c, slot):
            s_ref[slot] = scores(jnp.minimum(c, c_last), causal=True)[1]

        def consume(c, slot):
            k0 = pl.multiple_of(c * ck, ck)
            va = jnp.concatenate([vsl_ref[0, 0, pl.ds(k0, ck), :], ones], axis=1)
            acc_ref[...] += _dot(jnp.exp2(s_ref[slot]).astype(BF16), va)

        def pair(kk, carry):
            c = 2 * kk
            produce(c + 1, 1)
            consume(c, 0)
            produce(c + 2, 0)
            consume(c + 1, 1)
            return carry

        n_chunks = c_last + 1
        produce(0, 0)
        lax.fori_loop(0, n_chunks // 2, pair, 0)

        @pl.when(n_chunks % 2 == 1)
        def _():
            consume(c_last, 0)

        oslc_ref[...] = acc_ref[:, 0:HEAD_DIM] / acc_ref[:, HEAD_DIM:HEAD_DIM + 1]

    @pl.when(jnp.logical_not(safe))
    def _():
        set_lane_values(0.0)
        m_ref[...] = jnp.full(m_ref.shape, NEG_INF, F32)
        l_ref[...] = jnp.zeros(l_ref.shape, F32)
        acc_ref[...] = jnp.zeros(acc_ref.shape, F32)

        def chunk(c, causal):
            k0, sc = scores(c, causal)
            m_old = m_ref[...]
            m_new = jnp.maximum(m_old, jnp.max(sc, axis=-1, keepdims=True))
            alpha = jnp.exp2(m_old - m_new)
            pe = jnp.exp2(sc - m_new)
            l_ref[...] = alpha * l_ref[...] + jnp.sum(pe, axis=-1, keepdims=True)
            acc_ref[:, 0:HEAD_DIM] = alpha * acc_ref[:, 0:HEAD_DIM] + _dot(
                pe.astype(BF16), vsl_ref[0, 0, pl.ds(k0, ck), :])
            m_ref[...] = m_new

        over_chunks(chunk)
        oslc_ref[...] = acc_ref[:, 0:HEAD_DIM] / l_ref[...]

    o_slc = oslc_ref[...]

    wk = WINDOW + tq
    start = pl.multiple_of(jnp.maximum(i * tq - WINDOW, 0), tq)
    sw = _dot_nt(q_hi, kw_ref[0, 0, pl.ds(start, wk), :])
    kp = start + lax.broadcasted_iota(jnp.int32, (1, wk), 1)
    sw = jnp.where((kp <= t_row) & (kp > t_row - WINDOW), sw, NEG_INF)
    pw = jnp.exp2(sw - jnp.max(sw, axis=-1, keepdims=True))
    o_win = _dot(pw.astype(BF16), vw_ref[0, 0, pl.ds(start, wk), :]) / jnp.sum(pw, axis=-1, keepdims=True)

    gates = gate_ref[0]
    gn = gn_ref[...]
    n_gate = 3 * GROUP_R
    for r in range(GROUP_R):
        sl = slice(r * tq, (r + 1) * tq)

        def gate(branch, r=r):
            col = 3 * r + branch
            return jnp.where(gi == 0, gates[:, col:col + 1], gates[:, n_gate + col:n_gate + col + 1])

        o = gate(0) * o_cmp[sl] + gate(1) * o_slc[sl] + gate(2) * o_win[sl]
        hs = slice(r * HEAD_DIM, (r + 1) * HEAD_DIM)
        o_ref[0, :, hs] = _rms(o, gn[:, hs]).astype(o_ref.dtype)


def _attn(q_r, kc_hi, kc_lo, v_cmp, ksl, vsl, kw, vw, kmax, gates, gain, ovt):
    batch, _, seq, _ = q_r.shape
    nc = kc_hi.shape[1]
    nsp = ovt.shape[0]
    tq = Q_TILE
    rows = GROUP_R * tq
    gw = GROUP_R * HEAD_DIM
    cmp_spec = pl.BlockSpec((1, nc, HEAD_DIM), lambda b, g, i: (b * N_KV + g, 0, 0))
    kv_spec = pl.BlockSpec((1, 1, seq, HEAD_DIM), lambda b, g, i: (b, g, 0, 0))
    return pl.pallas_call(
        _attn_kernel,
        grid=(batch, N_KV, seq // tq),
        in_specs=[pl.BlockSpec((1, GROUP_R, tq, HEAD_DIM), lambda b, g, i: (b, g, i, 0)),
                  cmp_spec, cmp_spec, cmp_spec,
                  kv_spec, kv_spec, kv_spec, kv_spec,
                  pl.BlockSpec((1, 1, 8, LANES), lambda b, g, i: (b, g, 0, 0)),
                  pl.BlockSpec((1, tq, LANES), lambda b, g, i: (b, i, 0)),
                  pl.BlockSpec((1, gw), lambda b, g, i: (0, g)),
                  pl.BlockSpec((nsp, nc), lambda b, g, i: (0, 0))],
        out_specs=pl.BlockSpec((1, tq, gw), lambda b, g, i: (b, i, g)),
        out_shape=jax.ShapeDtypeStruct((batch, seq, N_KV * gw), BF16),
        scratch_shapes=[pltpu.VMEM((nsp // LANES, rows, 2 * HEAD_DIM), BF16),
                        pltpu.VMEM((rows, 1), F32), pltpu.VMEM((rows, 1), F32),
                        pltpu.VMEM((rows, 2 * HEAD_DIM), F32),
                        pltpu.VMEM((rows, HEAD_DIM), F32),
                        pltpu.VMEM((2, rows, SEL_CHUNK), F32),
                        pltpu.VMEM((rows, HEAD_DIM), F32),
                        pltpu.VMEM((nsp, tq), F32)],
        compiler_params=_cparams(("arbitrary", "arbitrary", "arbitrary")),
        name="attn",
    )(q_r, kc_hi, kc_lo, v_cmp, ksl, vsl, kw, vw, kmax, gates, gain, ovt)


def _outproj_kernel(yc_ref, ya_ref, x_ref, w_ref, ga_ref, g_ref, sh_ref, sc_ref, x1_ref, h2_ref):
    cw = yc_ref.shape[1]
    acc = _dot(yc_ref[...], w_ref[0:cw, :]) + _dot(ya_ref[...], w_ref[cw:, :])
    x1 = x_ref[...] + ga_ref[0] * acc
    x1_ref[...] = x1
    h2_ref[...] = (_rms(x1, g_ref[...]) * (1.0 + sc_ref[0]) + sh_ref[0]).astype(h2_ref.dtype)


def _outproj(y_conv, y_attn, x2, w_out, mod3, gain, seq):
    t, d = x2.shape
    tiles_per_seq = seq // ROW_TILE
    mod = lambda k: pl.BlockSpec((1, 1, d), lambda i: (i // tiles_per_seq, 0, k))
    row = lambda w: pl.BlockSpec((ROW_TILE, w), lambda i: (i, 0))
    return pl.pallas_call(
        _outproj_kernel,
        grid=(t // ROW_TILE,),
        in_specs=[row(y_conv.shape[1]), row(y_attn.shape[1]), row(d),
                  pl.BlockSpec(w_out.shape, lambda i: (0, 0)),
                  mod(2), pl.BlockSpec((1, d), lambda i: (0, 0)), mod(3), mod(4)],
        out_specs=[row(d), row(d)],
        out_shape=[jax.ShapeDtypeStruct((t, d), F32), jax.ShapeDtypeStruct((t, d), BF16)],
        compiler_params=_cparams(("arbitrary",)),
        name="outproj",
    )(y_conv, y_attn, x2, w_out, mod3, gain, mod3, mod3)


def _ffn_kernel(h_ref, w1_ref, w3_ref, w2_ref, x1_ref, gf_ref, o_ref):
    f = pl.program_id(1)
    last = pl.num_programs(1) - 1
    sub = h_ref.shape[0] // FFN_SUB

    def up(r):
        h = h_ref[r * sub:(r + 1) * sub, :]
        return _dot(h, w1_ref[...]), _dot(h, w3_ref[...])

    def down(r, ab):
        rs = slice(r * sub, (r + 1) * sub)
        o_ref[rs, :] += _dot((jax.nn.silu(ab[0]) * ab[1]).astype(BF16), w2_ref[...])

    @pl.when(f == 0)
    def _():
        o_ref[...] = jnp.zeros(o_ref.shape, F32)

    ab = up(0)
    for r in range(FFN_SUB):
        ab_next = up(r + 1) if r + 1 < FFN_SUB else None
        down(r, ab)
        ab = ab_next

    @pl.when(f == last)
    def _():
        o_ref[...] = x1_ref[...] + gf_ref[0] * o_ref[...]


def _ffn(h2, w1, w3, w2, x1, mod3, seq):
    t, d = x1.shape
    dff = w1.shape[1]
    tm = FFN_TM
    tiles_per_seq = seq // tm
    return pl.pallas_call(
        _ffn_kernel,
        grid=(t // tm, dff // FFN_TF),
        in_specs=[pl.BlockSpec((tm, d), lambda i, f: (i, 0)),
                  pl.BlockSpec((d, FFN_TF), lambda i, f: (0, f)),
                  pl.BlockSpec((d, FFN_TF), lambda i, f: (0, f)),
                  pl.BlockSpec((FFN_TF, d), lambda i, f: (f, 0)),
                  pl.BlockSpec((tm, d), lambda i, f: (i, 0), pipeline_mode=pl.Buffered(1)),
                  pl.BlockSpec((1, 1, d), lambda i, f: (i // tiles_per_seq, 0, 5))],
        out_specs=pl.BlockSpec((tm, d), lambda i, f: (i, 0)),
        out_shape=jax.ShapeDtypeStruct((t, d), F32),
        compiler_params=_cparams(("arbitrary", "arbitrary")),
        name="ffn",
    )(h2, w1, w3, w2, x1, mod3)


def _overlap_matrix(n_cmp_pad, n_sel, n_sel_pad):
    cmp_start = np.arange(n_cmp_pad) * CMP_STRIDE
    cmp_end = cmp_start + CMP_BLOCK - 1
    sel_start = np.arange(n_sel) * SEL_BLOCK
    ov = ((cmp_start[None, :] < sel_start[:, None] + SEL_BLOCK)
          & (cmp_end[None, :] >= sel_start[:, None])).astype(np.float32)
    return np.pad(ov, ((0, n_sel_pad - n_sel), (0, 0)))


def _w1_halves(w1):
    l, dk, hid = w1.shape
    half = l // 2
    return w1[:half].reshape(half * dk, hid), w1[half:].reshape(half * dk, hid)


def _pe_rows(pe):
    l, dk = pe.shape
    half = l // 2
    rows = jnp.stack([pe[:half].reshape(half * dk), pe[half:].reshape(half * dk)])
    return jnp.pad(rows, ((0, 6), (0, 0)))


def kernel(x, c, positions, ada_w, ada_b, norm_mix, norm_ffn, w_in, conv_w, cmp_pe_k, cmp_k_w1, cmp_k_b1, cmp_k_w2, cmp_pe_v, cmp_v_w1, cmp_v_b1, cmp_v_w2, q_norm, k_norm, out_norm_conv, out_norm_attn, w_out, ffn_w1, ffn_w3, ffn_w2):
    batch, seq, d = x.shape
    t = batch * seq
    depth = ada_w.shape[0]
    cw = conv_w.shape[2]
    aw = N_HEADS * HEAD_DIM
    kvw = N_KV * HEAD_DIM
    n_gate = 3 * N_HEADS
    assert seq % max(ROW_TILE, SEL_CHUNK, FFN_TM) == 0 and seq >= WINDOW + Q_TILE
    assert seq // SEL_BLOCK >= SEL_TOP_N and CMP_BLOCK == 2 * CMP_STRIDE
    nh = seq // CMP_STRIDE
    n_sel = seq // SEL_BLOCK
    n_sel_pad = -(-n_sel // LANES) * LANES
    n_sel_pad = 1 << (n_sel_pad - 1).bit_length()
    assert nh % CMP_TILE == 0

    inv = 1.0 / (ROPE_THETA ** (jnp.arange(0, HEAD_DIM, 2, dtype=F32) / HEAD_DIM))
    inv_full = jnp.concatenate([inv, inv])[None, :]
    sign = jnp.concatenate([-jnp.ones(HEAD_DIM // 2, F32), jnp.ones(HEAD_DIM // 2, F32)])[None, :]
    cosf, sinf = _rope_tables(positions.reshape(t, 1), inv_full, sign)
    at_end = lambda tab: jnp.pad(
        tab.reshape(batch, nh, CMP_STRIDE, HEAD_DIM)[:, 1:, CMP_STRIDE - 1], ((0, 0), (0, 1), (0, 0)))
    cosc, sinc = at_end(cosf), at_end(sinf)
    ovt = jnp.asarray(_overlap_matrix(nh, n_sel, n_sel_pad), BF16)

    c8 = jnp.pad(c, ((0, 8 - batch), (0, 0)))
    x2 = x.reshape(t, d)
    for l in range(depth):
        mod = _ada(c8, ada_w[l], ada_b[l][None, :])
        mod3 = mod[:batch].reshape(batch, 1, 6 * d)

        w = w_in[l]
        o_q = 3 * cw
        o_kc = o_q + aw
        o_rest = o_kc + kvw
        o_gl = o_rest + 5 * kvw
        main_w = o_q + 5 * kvw + n_gate
        main_pad = -(-main_w // INPROJ_TN) * INPROJ_TN
        w_main = jnp.concatenate(
            [w[:, :o_q], w[:, o_rest:o_gl], w[:, o_gl:o_gl + n_gate],
             jnp.zeros((d, main_pad - main_w), F32)], axis=1).astype(BF16)
        w_hi, w_lo = _split(w[:, o_q:o_rest])
        pm, pp = _inproj(x2, mod3, norm_mix[l][None, :], w_main, w_hi, w_lo, seq)

        y_conv = _conv(pm, conv_w[l], out_norm_conv[l][None, :], seq, cw)

        q_r, kc_t, vc_t, ksl, vsl, kw, vw, gates, kmax = _prep(
            pm, pp, cosf, sinf, q_norm[l][None, :], k_norm[l], batch, seq, o_q, o_q + 5 * kvw)

        wka, wkb = _w1_halves(cmp_k_w1[l])
        wva, wvb = _w1_halves(cmp_v_w1[l])
        kc_hi, kc_lo, v_cmp = _compress(
            kc_t.reshape(batch * N_KV, nh, CMP_STRIDE * HEAD_DIM),
            vc_t.reshape(batch * N_KV, nh, CMP_STRIDE * HEAD_DIM),
            _pe_rows(cmp_pe_k[l]), (*_split(wka), *_split(wkb)), cmp_k_b1[l][None, :],
            _split(cmp_k_w2[l]),
            _pe_rows(cmp_pe_v[l]), (wva.astype(BF16), wvb.astype(BF16)), cmp_v_b1[l][None, :],
            cmp_v_w2[l].astype(BF16), k_norm[l], cosc, sinc)

        y_attn = _attn(q_r, kc_hi, kc_lo, v_cmp, ksl, vsl, kw, vw, kmax, gates,
                       out_norm_attn[l][None, :], ovt)

        x1, h2 = _outproj(y_conv, y_attn.reshape(t, aw), x2, w_out[l].astype(BF16), mod3,
                          norm_ffn[l][None, :], seq)
        x2 = _ffn(h2, ffn_w1[l].astype(BF16), ffn_w3[l].astype(BF16), ffn_w2[l].astype(BF16),
                  x1, mod3, seq)
    return x2.reshape(batch, seq, d)
```

```python
import functools

import numpy as np
import jax
import jax.numpy as jnp
from jax import lax
from jax.experimental import pallas as pl
from jax.experimental.pallas import tpu as pltpu

F32 = jnp.float32
BF16 = jnp.bfloat16

CONV_GROUP = 128
CONV_KSIZE = 3
HEAD_DIM = 128
N_KV = 2
GROUP_R = 4
N_HEADS = N_KV * GROUP_R
CMP_BLOCK = 32
CMP_STRIDE = 16
SEL_BLOCK = 64
SEL_TOP_N = 16
WINDOW = 512
ROPE_THETA = 10000.0
EPS = 1e-6
NEG_INF = -1e30
LOG2E = 1.4426950408889634
SHIFT_MARGIN = 57.0
SAFE_BOUND = 86.0

LANES = 128
VMEM_LIMIT = 56 * 1024 * 1024

ROW_TILE = 512
Q_TILE = 256
SEL_CHUNK = 512
CMP_TILE = 256
ADA_TN = 1024
INPROJ_SUB = 1
MXU_TILE = 256
FFN_TF = 512
FFN_TM = 1024
FFN_SUB = 2


def _cparams(sem):
    return pltpu.CompilerParams(dimension_semantics=sem, vmem_limit_bytes=VMEM_LIMIT)


def _split(a):
    hi = a.astype(BF16)
    lo = (a - hi.astype(F32)).astype(BF16)
    return hi, lo


def _dot(a, b):
    return jnp.dot(a, b, preferred_element_type=F32)


def _dot_nt(a, b):
    return lax.dot_general(a, b, (((1,), (1,)), ((), ())), preferred_element_type=F32)


def _dot3(a_hi, a_lo, b_hi, b_lo, dot=_dot):
    return (dot(a_hi, b_lo) + dot(a_lo, b_hi)) + dot(a_hi, b_hi)


def _rms(x, gain):
    return x * lax.rsqrt(jnp.mean(x * x, axis=-1, keepdims=True) + EPS) * gain


def _rope(y, cosf, sinf):
    return y * cosf + pltpu.roll(y, HEAD_DIM // 2, 1) * sinf


def _ada_kernel(c_ref, w_ref, b_ref, o_ref):
    s_hi, s_lo = _split(jax.nn.silu(c_ref[...]))
    w_hi, w_lo = _split(w_ref[...])
    o_ref[...] = _dot3(s_hi, s_lo, w_hi, w_lo) + b_ref[...]


def _ada(c8, w, b):
    d, n = w.shape
    return pl.pallas_call(
        _ada_kernel,
        grid=(n // ADA_TN,),
        in_specs=[pl.BlockSpec((8, d), lambda j: (0, 0)),
                  pl.BlockSpec((d, ADA_TN), lambda j: (0, j)),
                  pl.BlockSpec((1, ADA_TN), lambda j: (0, j))],
        out_specs=pl.BlockSpec((8, ADA_TN), lambda j: (0, j)),
        out_shape=jax.ShapeDtypeStruct((8, n), F32),
        compiler_params=_cparams(("arbitrary",)),
        name="ada",
    )(c8, w, b)


def _inproj_kernel(x_ref, sh_ref, sc_ref, g_ref, w_ref, o_ref):
    sub = x_ref.shape[0] // INPROJ_SUB

    def norm(r):
        x = x_ref[r * sub:(r + 1) * sub, :]
        return (_rms(x, g_ref[...]) * (1.0 + sc_ref[0]) + sh_ref[0]).astype(BF16)

    h = norm(0)
    for r in range(INPROJ_SUB):
        h_next = norm(r + 1) if r + 1 < INPROJ_SUB else None
        o_ref[r * sub:(r + 1) * sub, :] = _dot(h, w_ref[...]).astype(o_ref.dtype)
        h = h_next


def _inproj(x2, mod3, gain, w, seq):
    t, d = x2.shape
    n = w.shape[1]
    tiles_per_seq = seq // ROW_TILE
    return pl.pallas_call(
        _inproj_kernel,
        grid=(t // ROW_TILE,),
        in_specs=[
            pl.BlockSpec((ROW_TILE, d), lambda i: (i, 0)),
            pl.BlockSpec((1, 1, d), lambda i: (i // tiles_per_seq, 0, 0)),
            pl.BlockSpec((1, 1, d), lambda i: (i // tiles_per_seq, 0, 1)),
            pl.BlockSpec((1, d), lambda i: (0, 0)),
            pl.BlockSpec((d, n), lambda i: (0, 0), pipeline_mode=pl.Buffered(1)),
        ],
        out_specs=pl.BlockSpec((ROW_TILE, n), lambda i: (i, 0)),
        out_shape=jax.ShapeDtypeStruct((t, n), BF16),
        compiler_params=_cparams(("arbitrary",)),
        name="inproj",
    )(x2, mod3, mod3, gain, w)


HALO = 16


def _conv_kernel(cb_ref, cc_ref, ch_ref, ccp_ref, chp_ref, w_ref, g_ref, o_ref, *, tiles_per_seq):
    i = pl.program_id(0)
    u = cc_ref[...].astype(F32) * ch_ref[...].astype(F32)
    up = ccp_ref[...].astype(F32) * chp_ref[...].astype(F32)
    up = jnp.where(i % tiles_per_seq == 0, 0.0, up)
    rows = lax.broadcasted_iota(jnp.int32, u.shape, 0)
    u1 = jnp.where(rows == 0, up[HALO - 1:HALO], pltpu.roll(u, 1, 0))
    u2 = jnp.where(rows == 0, up[HALO - 2:HALO - 1],
                   jnp.where(rows == 1, up[HALO - 1:HALO], pltpu.roll(u, 2, 0)))
    w = w_ref[...]
    y = cb_ref[...].astype(F32) * (w[2:3] * u + w[1:2] * u1 + w[0:1] * u2)
    gain = g_ref[...]
    for g in range(y.shape[1] // CONV_GROUP):
        sl = slice(g * CONV_GROUP, (g + 1) * CONV_GROUP)
        o_ref[:, sl] = _rms(y[:, sl], gain[:, sl]).astype(o_ref.dtype)


def _conv(pm, conv_w, gain, seq, cw):
    t = pm.shape[0]
    tiles_per_seq = seq // ROW_TILE
    hb = ROW_TILE // HALO

    def prev(col):
        return pl.BlockSpec((HALO, cw), lambda i: (jnp.maximum(i * hb - 1, 0), col))

    return pl.pallas_call(
        functools.partial(_conv_kernel, tiles_per_seq=tiles_per_seq),
        grid=(t // ROW_TILE,),
        in_specs=[pl.BlockSpec((ROW_TILE, cw), lambda i: (i, 0)),
                  pl.BlockSpec((ROW_TILE, cw), lambda i: (i, 1)),
                  pl.BlockSpec((ROW_TILE, cw), lambda i: (i, 2)),
                  prev(1), prev(2),
                  pl.BlockSpec((CONV_KSIZE, cw), lambda i: (0, 0)),
                  pl.BlockSpec((1, cw), lambda i: (0, 0))],
        out_specs=pl.BlockSpec((ROW_TILE, cw), lambda i: (i, 0)),
        out_shape=jax.ShapeDtypeStruct((t, cw), BF16),
        compiler_params=_cparams(("arbitrary",)),
        name="conv",
    )(pm, pm, pm, pm, pm, conv_w, gain)


def _rope_kernel(pos_ref, inv_ref, sign_ref, cos_ref, sin_ref):
    ang = pos_ref[...].astype(F32) * inv_ref[...]
    cos_ref[...] = jnp.cos(ang)
    sin_ref[...] = jnp.sin(ang) * sign_ref[...]


def _rope_tables(pos2, inv_full, sign):
    t = pos2.shape[0]
    ts = 2048
    return pl.pallas_call(
        _rope_kernel,
        grid=(t // ts,),
        in_specs=[pl.BlockSpec((ts, 1), lambda i: (i, 0)),
                  pl.BlockSpec((1, HEAD_DIM), lambda i: (0, 0)),
                  pl.BlockSpec((1, HEAD_DIM), lambda i: (0, 0))],
        out_specs=[pl.BlockSpec((ts, HEAD_DIM), lambda i: (i, 0))] * 2,
        out_shape=[jax.ShapeDtypeStruct((t, HEAD_DIM), F32)] * 2,
        compiler_params=_cparams(("arbitrary",)),
        name="rope",
    )(pos2, inv_full, sign)


def _prep_kernel(q_ref, kc_ref, vc_ref, ksl_ref, vsl_ref, kw_ref, vw_ref, gl_ref, cos_ref, sin_ref,
                 qn_ref, kn_ref,
                 q_out, kc_out, vc_out, ksl_out, vsl_out, kw_out, vw_out, gate_out, kmax_out):
    cosf = cos_ref[...]
    sinf = sin_ref[...]
    scale = HEAD_DIM ** -0.5 * LOG2E
    qn = qn_ref[...]
    kn = kn_ref[...]
    for h in range(N_HEADS):
        qh = q_ref[:, h * HEAD_DIM:(h + 1) * HEAD_DIM].astype(F32)
        q_out[0, h] = (_rope(_rms(qh, qn), cosf, sinf) * scale).astype(BF16)
    for g in range(N_KV):
        sl = slice(g * HEAD_DIM, (g + 1) * HEAD_DIM)
        kc_out[0, g] = kc_ref[:, sl]
        vc_out[0, g] = vc_ref[:, sl]
        vsl_out[0, g] = vsl_ref[:, sl]
        vw_out[0, g] = vw_ref[:, sl]
        ksl = _rope(_rms(ksl_ref[:, sl].astype(F32), kn[1:2]), cosf, sinf).astype(BF16)
        ksl_out[0, g] = ksl
        kw_out[0, g] = _rope(_rms(kw_ref[:, sl].astype(F32), kn[2:3]), cosf, sinf).astype(BF16)
        kf = ksl.astype(F32)
        n2 = jnp.full(kmax_out.shape[2:], jnp.max(jnp.sum(kf * kf, axis=-1, keepdims=True)), F32)

        @pl.when(pl.program_id(1) == 0)
        def _(g=g, n2=n2):
            kmax_out[0, g] = n2

        @pl.when(pl.program_id(1) > 0)
        def _(g=g, n2=n2):
            kmax_out[0, g] = jnp.maximum(kmax_out[0, g], n2)
    gate_out[0] = jax.nn.sigmoid(gl_ref[...].astype(F32))


def _prep(pm, cosf, sinf, q_norm, k_norm, batch, seq, q_col, gl_col):
    ts = ROW_TILE
    tiles = seq // ts
    qw = N_HEADS * HEAD_DIM
    kvw = N_KV * HEAD_DIM
    kvb = (q_col + qw) // kvw

    def rows(width, col):
        return pl.BlockSpec((ts, width), lambda b, s: (b * tiles + s, col))

    def heads(n):
        return pl.BlockSpec((1, n, ts, HEAD_DIM), lambda b, s: (b, 0, s, 0))

    kv_shape = jax.ShapeDtypeStruct((batch, N_KV, seq, HEAD_DIM), BF16)
    return pl.pallas_call(
        _prep_kernel,
        grid=(batch, tiles),
        in_specs=[rows(qw, q_col // qw)] + [rows(kvw, kvb + n) for n in range(6)] + [
                  rows(LANES, gl_col // LANES),
                  rows(HEAD_DIM, 0), rows(HEAD_DIM, 0),
                  pl.BlockSpec((1, HEAD_DIM), lambda b, s: (0, 0)),
                  pl.BlockSpec((3, HEAD_DIM), lambda b, s: (0, 0))],
        out_specs=[heads(N_HEADS)] + [heads(N_KV)] * 6 + [
                   pl.BlockSpec((1, ts, LANES), lambda b, s: (b, s, 0)),
                   pl.BlockSpec((1, N_KV, 8, LANES), lambda b, s: (b, 0, 0, 0))],
        out_shape=[jax.ShapeDtypeStruct((batch, N_HEADS, seq, HEAD_DIM), BF16)] + [kv_shape] * 6 + [
                   jax.ShapeDtypeStruct((batch, seq, LANES), F32),
                   jax.ShapeDtypeStruct((batch, N_KV, 8, LANES), F32)],
        compiler_params=_cparams(("arbitrary", "arbitrary")),
        name="prep",
    )(*([pm] * 8), cosf, sinf, q_norm, k_norm)


def _shift_up(h2, h2_next):
    n = h2.shape[0]
    rows = lax.broadcasted_iota(jnp.int32, h2.shape, 0)
    return jnp.where(rows == n - 1, h2_next[0:1], pltpu.roll(h2, n - 1, 0))


def _compress_mlp(a, a_next, pe, w1a, w1b, b1, w2):
    h1 = _dot(a, w1a)
    h2 = _shift_up(_dot(a, w1b), _dot(a_next, w1b))
    pe_term = _dot(pe, w1a)[0:1] + _dot(pe, w1b)[1:2]
    hid = jax.nn.gelu(h1 + h2 + pe_term + b1, approximate=True)
    return _dot(hid.astype(BF16), w2)


def _compress_kernel(ak_ref, akn_ref, av_ref, avn_ref,
                     pek_ref, wka, wkb, bk_ref, wk2,
                     pev_ref, wva, wvb, bv_ref, wv2,
                     kn_ref, cos_ref, sin_ref,
                     k_out, v_out):
    kc = _compress_mlp(ak_ref[0], akn_ref[0], pek_ref[...].astype(BF16), wka[...], wkb[...],
                       bk_ref[...], wk2[...])
    k_out[0] = _rope(_rms(kc, kn_ref[0:1]), cos_ref[0], sin_ref[0]).astype(BF16)
    v_out[0] = _compress_mlp(av_ref[0], avn_ref[0], pev_ref[...].astype(BF16), wva[...], wvb[...],
                             bv_ref[...], wv2[...]).astype(BF16)


def _compress(ak, av, pek2, wk, bk, wk2, pev2, wv, bv, wv2, k_norm, cosc, sinc):
    bg, nh, kk = ak.shape
    hid = bk.shape[1]
    tiles = nh // CMP_TILE
    cur = pl.BlockSpec((1, CMP_TILE, kk), lambda g, t: (g, t, 0))
    per = CMP_TILE // 16
    last = nh // 16 - 1
    nxt = pl.BlockSpec((1, 16, kk), lambda g, t: (g, jnp.minimum((t + 1) * per, last), 0))
    full = lambda shp: pl.BlockSpec(shp, lambda g, t: tuple(0 for _ in shp))
    mlp = [full((8, kk)), full((kk, hid)), full((kk, hid)), full((1, hid)), full((hid, HEAD_DIM))]
    out = pl.BlockSpec((1, CMP_TILE, HEAD_DIM), lambda g, t: (g, t, 0))
    tab = pl.BlockSpec((1, CMP_TILE, HEAD_DIM), lambda g, t: (g // N_KV, t, 0))
    oshape = jax.ShapeDtypeStruct((bg, nh, HEAD_DIM), BF16)
    return pl.pallas_call(
        _compress_kernel,
        grid=(bg, tiles),
        in_specs=[cur, nxt, cur, nxt] + mlp + mlp + [full((3, HEAD_DIM)), tab, tab],
        out_specs=[out, out],
        out_shape=[oshape, oshape],
        compiler_params=_cparams(("arbitrary", "arbitrary")),
        name="compress",
    )(ak, ak, av, av, pek2, *wk, bk, wk2, pev2, *wv, bv, wv2, k_norm, cosc, sinc)


def _count(mask):
    return jnp.sum(jnp.where(mask, 1.0, 0.0), axis=0, keepdims=True)


def _select_blocks(imp, t_pos):
    n_blocks = imp.shape[0]
    j = lax.broadcasted_iota(jnp.int32, (n_blocks, 1), 0)
    cur = jnp.right_shift(t_pos, SEL_BLOCK.bit_length() - 1)
    valid = j * SEL_BLOCK <= t_pos
    forced = (j == 0) | (j == cur) | (j == cur - 1)
    cand = valid & jnp.logical_not(forced)
    n_forced = 1.0 + jnp.where(cur >= 1, 1.0, 0.0) + jnp.where(cur >= 2, 1.0, 0.0)
    k = float(SEL_TOP_N) - n_forced

    imp_c = jnp.where(cand, imp, -1.0)

    def value_step(it, tb):
        trial = tb | jnp.left_shift(jnp.int32(1), 30 - it)
        keep = _count(imp_c >= pltpu.bitcast(trial, F32)) >= k
        return jnp.where(keep, trial, tb)

    tb = lax.fori_loop(0, 31, value_step, jnp.zeros(t_pos.shape, jnp.int32))
    thr = pltpu.bitcast(tb, F32)
    above = imp_c > thr
    need = k - _count(above)

    j_tied = jnp.where(imp_c == thr, j, 2 * n_blocks)

    def index_step(it, jb):
        trial = jb | jnp.left_shift(jnp.int32(1), (n_blocks.bit_length() - 1) - it)
        keep = _count(j_tied < trial) < need
        return jnp.where(keep, trial, jb)

    def index_search():
        return lax.fori_loop(0, n_blocks.bit_length(), index_step,
                             jnp.zeros(t_pos.shape, jnp.int32))

    all_tied_taken = jnp.min(need - _count(j_tied < 2 * n_blocks)) >= 0.0
    jb = lax.cond(all_tied_taken,
                  lambda: jnp.full(t_pos.shape, 2 * n_blocks - 1, jnp.int32), index_search)
    return jnp.where((forced & valid) | above | (j_tied <= jb), 1.0, 0.0)


def _attn_kernel(q_ref, kc_ref, vc_ref, ksl_ref, vsl_ref, kw_ref, vw_ref, kmax_ref,
                 gate_ref, gn_ref, ovt_ref, o_ref, qa_ref, m_ref, l_ref, acc_ref, oslc_ref, s_ref,
                 ocmp_ref, impt_ref):
    gi = pl.program_id(1)
    i = pl.program_id(2)
    tq = q_ref.shape[2]
    rows = GROUP_R * tq
    nc = kc_ref.shape[1]
    nsp = ovt_ref.shape[0]
    ck = SEL_CHUNK
    blocks_per_chunk = ck // SEL_BLOCK

    q_hi = q_ref[0].reshape(rows, HEAD_DIM)
    t_row = i * tq + jnp.bitwise_and(lax.broadcasted_iota(jnp.int32, (rows, 1), 0), tq - 1)

    def compressed(ncols):
        s = _dot_nt(q_hi, kc_ref[0, 0:ncols, :])
        n_end = lax.broadcasted_iota(jnp.int32, (1, ncols), 1) * CMP_STRIDE + (CMP_BLOCK - 1)
        s = jnp.where(n_end <= t_row, s, NEG_INF)
        m = jnp.max(s, axis=-1, keepdims=True)
        p = jnp.exp2(s - m)
        l = jnp.sum(p, axis=-1, keepdims=True)
        pc = p * jnp.where(m > 0.5 * NEG_INF, 1.0 / l, 0.0)
        ocmp_ref[...] = _dot(pc.astype(BF16), vc_ref[0, 0:ncols, :])
        psum = pc[0:tq]
        for r in range(1, GROUP_R):
            psum = psum + pc[r * tq:(r + 1) * tq]
        impt_ref[...] = _dot_nt(ovt_ref[:, 0:ncols], psum.astype(BF16))

    first_half = 2 * i < pl.num_programs(2)

    @pl.when(first_half)
    def _():
        compressed(nc // 2)

    @pl.when(jnp.logical_not(first_half))
    def _():
        compressed(nc)

    o_cmp = ocmp_ref[...]

    t_pos = i * tq + lax.broadcasted_iota(jnp.int32, (1, tq), 1)
    sel = jnp.transpose(_select_blocks(impt_ref[...], t_pos))
    sel = jnp.concatenate([sel] * GROUP_R, axis=0) > 0.5

    def augmented_keys(c):
        k0 = pl.multiple_of(c * ck, ck)
        kblk = c * blocks_per_chunk + jnp.right_shift(
            lax.broadcasted_iota(jnp.int32, (ck, 1), 0), SEL_BLOCK.bit_length() - 1)
        lane = lax.broadcasted_iota(jnp.int32, (1, LANES), 1)
        onehot = jnp.where(jnp.bitwise_and(kblk, LANES - 1) == lane, 1.0, 0.0).astype(BF16)
        return k0, jnp.concatenate([ksl_ref[0, 0, pl.ds(k0, ck), :], onehot], axis=1)

    def scores(c, causal):
        k0, ka = augmented_keys(c)
        sc = _dot_nt(qa_ref[c // (LANES // blocks_per_chunk)], ka)
        if causal:
            kp = c * ck + lax.broadcasted_iota(jnp.int32, (1, ck), 1)
            sc = jnp.where(kp <= t_row, sc, NEG_INF)
        return k0, sc

    def set_lane_values(selected_value):
        lanev = jnp.where(sel, selected_value, NEG_INF).astype(BF16)
        for w in range(nsp // LANES):
            qa_ref[w] = jnp.concatenate([q_hi, lanev[:, w * LANES:(w + 1) * LANES]], axis=1)

    c_last = (i * tq) // ck

    def over_chunks(chunk):
        def past(c, carry):
            chunk(c, causal=False)
            return carry
        lax.fori_loop(0, c_last, past, 0)
        chunk(c_last, causal=True)

    qf = q_hi.astype(F32)
    bound = jnp.sqrt(jnp.sum(qf * qf, axis=-1, keepdims=True) * kmax_ref[0, 0][0:1, 0:1]) * 1.01
    safe = jnp.max(bound) <= SAFE_BOUND

    @pl.when(safe)
    def _():
        set_lane_values(SHIFT_MARGIN - bound)
        acc_ref[...] = jnp.zeros(acc_ref.shape, F32)
        ones = jnp.ones((ck, HEAD_DIM), BF16)

        def produce(c, slot):
            s_ref[slot] = scores(jnp.minimum(c, c_last), causal=True)[1]

        def consume(c, slot):
            k0 = pl.multiple_of(c * ck, ck)
            va = jnp.concatenate([vsl_ref[0, 0, pl.ds(k0, ck), :], ones], axis=1)
            acc_ref[...] += _dot(jnp.exp2(s_ref[slot]).astype(BF16), va)

        def pair(kk, carry):
            c = 2 * kk
            produce(c + 1, 1)
            consume(c, 0)
            produce(c + 2, 0)
            consume(c + 1, 1)
            return carry

        n_chunks = c_last + 1
        produce(0, 0)
        lax.fori_loop(0, n_chunks // 2, pair, 0)

        @pl.when(n_chunks % 2 == 1)
        def _():
            consume(c_last, 0)

        oslc_ref[...] = acc_ref[:, 0:HEAD_DIM] / acc_ref[:, HEAD_DIM:HEAD_DIM + 1]

    @pl.when(jnp.logical_not(safe))
    def _():
        set_lane_values(0.0)
        m_ref[...] = jnp.full(m_ref.shape, NEG_INF, F32)
        l_ref[...] = jnp.zeros(l_ref.shape, F32)
        acc_ref[...] = jnp.zeros(acc_ref.shape, F32)

        def chunk(c, causal):
            k0, sc = scores(c, causal)
            m_old = m_ref[...]
            m_new = jnp.maximum(m_old, jnp.max(sc, axis=-1, keepdims=True))
            alpha = jnp.exp2(m_old - m_new)
            pe = jnp.exp2(sc - m_new)
            l_ref[...] = alpha * l_ref[...] + jnp.sum(pe, axis=-1, keepdims=True)
            acc_ref[:, 0:HEAD_DIM] = alpha * acc_ref[:, 0:HEAD_DIM] + _dot(
                pe.astype(BF16), vsl_ref[0, 0, pl.ds(k0, ck), :])
            m_ref[...] = m_new

        over_chunks(chunk)
        oslc_ref[...] = acc_ref[:, 0:HEAD_DIM] / l_ref[...]

    o_slc = oslc_ref[...]

    wk = WINDOW + tq
    start = pl.multiple_of(jnp.maximum(i * tq - WINDOW, 0), tq)
    sw = _dot_nt(q_hi, kw_ref[0, 0, pl.ds(start, wk), :])
    kp = start + lax.broadcasted_iota(jnp.int32, (1, wk), 1)
    sw = jnp.where((kp <= t_row) & (kp > t_row - WINDOW), sw, NEG_INF)
    pw = jnp.exp2(sw - jnp.max(sw, axis=-1, keepdims=True))
    o_win = _dot(pw.astype(BF16), vw_ref[0, 0, pl.ds(start, wk), :]) / jnp.sum(pw, axis=-1, keepdims=True)

    gates = gate_ref[0]
    gn = gn_ref[...]
    n_gate = 3 * GROUP_R
    for r in range(GROUP_R):
        sl = slice(r * tq, (r + 1) * tq)

        def gate(branch, r=r):
            col = 3 * r + branch
            return jnp.where(gi == 0, gates[:, col:col + 1], gates[:, n_gate + col:n_gate + col + 1])

        o = gate(0) * o_cmp[sl] + gate(1) * o_slc[sl] + gate(2) * o_win[sl]
        hs = slice(r * HEAD_DIM, (r + 1) * HEAD_DIM)
        o_ref[0, :, hs] = _rms(o, gn[:, hs]).astype(o_ref.dtype)


def _attn(q_r, k_cmp, v_cmp, ksl, vsl, kw, vw, kmax, gates, gain, ovt):
    batch, _, seq, _ = q_r.shape
    nc = k_cmp.shape[1]
    nsp = ovt.shape[0]
    tq = Q_TILE
    rows = GROUP_R * tq
    gw = GROUP_R * HEAD_DIM
    cmp_spec = pl.BlockSpec((1, nc, HEAD_DIM), lambda b, g, i: (b * N_KV + g, 0, 0))
    kv_spec = pl.BlockSpec((1, 1, seq, HEAD_DIM), lambda b, g, i: (b, g, 0, 0))
    return pl.pallas_call(
        _attn_kernel,
        grid=(batch, N_KV, seq // tq),
        in_specs=[pl.BlockSpec((1, GROUP_R, tq, HEAD_DIM), lambda b, g, i: (b, g, i, 0)),
                  cmp_spec, cmp_spec,
                  kv_spec, kv_spec, kv_spec, kv_spec,
                  pl.BlockSpec((1, 1, 8, LANES), lambda b, g, i: (b, g, 0, 0)),
                  pl.BlockSpec((1, tq, LANES), lambda b, g, i: (b, i, 0)),
                  pl.BlockSpec((1, gw), lambda b, g, i: (0, g)),
                  pl.BlockSpec((nsp, nc), lambda b, g, i: (0, 0))],
        out_specs=pl.BlockSpec((1, tq, gw), lambda b, g, i: (b, i, g)),
        out_shape=jax.ShapeDtypeStruct((batch, seq, N_KV * gw), BF16),
        scratch_shapes=[pltpu.VMEM((nsp // LANES, rows, 2 * HEAD_DIM), BF16),
                        pltpu.VMEM((rows, 1), F32), pltpu.VMEM((rows, 1), F32),
                        pltpu.VMEM((rows, 2 * HEAD_DIM), F32),
                        pltpu.VMEM((rows, HEAD_DIM), F32),
                        pltpu.VMEM((2, rows, SEL_CHUNK), F32),
                        pltpu.VMEM((rows, HEAD_DIM), F32),
                        pltpu.VMEM((nsp, tq), F32)],
        compiler_params=_cparams(("arbitrary", "arbitrary", "arbitrary")),
        name="attn",
    )(q_r, k_cmp, v_cmp, ksl, vsl, kw, vw, kmax, gates, gain, ovt)


def _outproj_kernel(yc_ref, ya_ref, x_ref, w_ref, ga_ref, g_ref, sh_ref, sc_ref, x1_ref, h2_ref):
    cw = yc_ref.shape[1]
    acc = _dot(yc_ref[...], w_ref[0:cw, :]) + _dot(ya_ref[...], w_ref[cw:, :])
    x1 = x_ref[...] + ga_ref[0] * acc
    x1_ref[...] = x1
    h2_ref[...] = (_rms(x1, g_ref[...]) * (1.0 + sc_ref[0]) + sh_ref[0]).astype(h2_ref.dtype)


def _outproj(y_conv, y_attn, x2, w_out, mod3, gain, seq):
    t, d = x2.shape
    tiles_per_seq = seq // ROW_TILE
    mod = lambda k: pl.BlockSpec((1, 1, d), lambda i: (i // tiles_per_seq, 0, k))
    row = lambda w: pl.BlockSpec((ROW_TILE, w), lambda i: (i, 0))
    return pl.pallas_call(
        _outproj_kernel,
        grid=(t // ROW_TILE,),
        in_specs=[row(y_conv.shape[1]), row(y_attn.shape[1]), row(d),
                  pl.BlockSpec(w_out.shape, lambda i: (0, 0)),
                  mod(2), pl.BlockSpec((1, d), lambda i: (0, 0)), mod(3), mod(4)],
        out_specs=[row(d), row(d)],
        out_shape=[jax.ShapeDtypeStruct((t, d), F32), jax.ShapeDtypeStruct((t, d), BF16)],
        compiler_params=_cparams(("arbitrary",)),
        name="outproj",
    )(y_conv, y_attn, x2, w_out, mod3, gain, mod3, mod3)


def _ffn_kernel(h_ref, w1_ref, w3_ref, w2_ref, x1_ref, gf_ref, o_ref):
    f = pl.program_id(1)
    last = pl.num_programs(1) - 1
    sub = h_ref.shape[0] // FFN_SUB

    def up(r):
        h = h_ref[r * sub:(r + 1) * sub, :]
        return _dot(h, w1_ref[...]), _dot(h, w3_ref[...])

    def down(r, ab):
        rs = slice(r * sub, (r + 1) * sub)
        o_ref[rs, :] += _dot((jax.nn.silu(ab[0]) * ab[1]).astype(BF16), w2_ref[...])

    @pl.when(f == 0)
    def _():
        o_ref[...] = jnp.zeros(o_ref.shape, F32)

    ab = up(0)
    for r in range(FFN_SUB):
        ab_next = up(r + 1) if r + 1 < FFN_SUB else None
        down(r, ab)
        ab = ab_next

    @pl.when(f == last)
    def _():
        o_ref[...] = x1_ref[...] + gf_ref[0] * o_ref[...]


def _ffn(h2, w1, w3, w2, x1, mod3, seq):
    t, d = x1.shape
    dff = w1.shape[1]
    tm = FFN_TM
    tiles_per_seq = seq // tm
    return pl.pallas_call(
        _ffn_kernel,
        grid=(t // tm, dff // FFN_TF),
        in_specs=[pl.BlockSpec((tm, d), lambda i, f: (i, 0)),
                  pl.BlockSpec((d, FFN_TF), lambda i, f: (0, f)),
                  pl.BlockSpec((d, FFN_TF), lambda i, f: (0, f)),
                  pl.BlockSpec((FFN_TF, d), lambda i, f: (f, 0)),
                  pl.BlockSpec((tm, d), lambda i, f: (i, 0), pipeline_mode=pl.Buffered(1)),
                  pl.BlockSpec((1, 1, d), lambda i, f: (i // tiles_per_seq, 0, 5))],
        out_specs=pl.BlockSpec((tm, d), lambda i, f: (i, 0)),
        out_shape=jax.ShapeDtypeStruct((t, d), F32),
        compiler_params=_cparams(("arbitrary", "arbitrary")),
        name="ffn",
    )(h2, w1, w3, w2, x1, mod3)


def _overlap_matrix(n_cmp_pad, n_sel, n_sel_pad):
    cmp_start = np.arange(n_cmp_pad) * CMP_STRIDE
    cmp_end = cmp_start + CMP_BLOCK - 1
    sel_start = np.arange(n_sel) * SEL_BLOCK
    ov = ((cmp_start[None, :] < sel_start[:, None] + SEL_BLOCK)
          & (cmp_end[None, :] >= sel_start[:, None])).astype(np.float32)
    return np.pad(ov, ((0, n_sel_pad - n_sel), (0, 0)))


def _w1_halves(w1):
    l, dk, hid = w1.shape
    half = l // 2
    return w1[:half].reshape(half * dk, hid), w1[half:].reshape(half * dk, hid)


def _pe_rows(pe):
    l, dk = pe.shape
    half = l // 2
    rows = jnp.stack([pe[:half].reshape(half * dk), pe[half:].reshape(half * dk)])
    return jnp.pad(rows, ((0, 6), (0, 0)))


def kernel(x, c, positions, ada_w, ada_b, norm_mix, norm_ffn, w_in, conv_w, cmp_pe_k, cmp_k_w1, cmp_k_b1, cmp_k_w2, cmp_pe_v, cmp_v_w1, cmp_v_b1, cmp_v_w2, q_norm, k_norm, out_norm_conv, out_norm_attn, w_out, ffn_w1, ffn_w3, ffn_w2):
    batch, seq, d = x.shape
    t = batch * seq
    depth = ada_w.shape[0]
    cw = conv_w.shape[2]
    aw = N_HEADS * HEAD_DIM
    kvw = N_KV * HEAD_DIM
    n_gate = 3 * N_HEADS
    assert seq % max(ROW_TILE, SEL_CHUNK, FFN_TM) == 0 and seq >= WINDOW + Q_TILE
    assert seq // SEL_BLOCK >= SEL_TOP_N and CMP_BLOCK == 2 * CMP_STRIDE
    nh = seq // CMP_STRIDE
    n_sel = seq // SEL_BLOCK
    n_sel_pad = -(-n_sel // LANES) * LANES
    n_sel_pad = 1 << (n_sel_pad - 1).bit_length()
    assert nh % CMP_TILE == 0

    inv = 1.0 / (ROPE_THETA ** (jnp.arange(0, HEAD_DIM, 2, dtype=F32) / HEAD_DIM))
    inv_full = jnp.concatenate([inv, inv])[None, :]
    sign = jnp.concatenate([-jnp.ones(HEAD_DIM // 2, F32), jnp.ones(HEAD_DIM // 2, F32)])[None, :]
    cosf, sinf = _rope_tables(positions.reshape(t, 1), inv_full, sign)
    at_end = lambda tab: jnp.pad(
        tab.reshape(batch, nh, CMP_STRIDE, HEAD_DIM)[:, 1:, CMP_STRIDE - 1], ((0, 0), (0, 1), (0, 0)))
    cosc, sinc = at_end(cosf), at_end(sinf)
    ovt = jnp.asarray(_overlap_matrix(nh, n_sel, n_sel_pad), BF16)

    c8 = jnp.pad(c, ((0, 8 - batch), (0, 0)))
    x2 = x.reshape(t, d)
    for l in range(depth):
        mod = _ada(c8, ada_w[l], ada_b[l][None, :])
        mod3 = mod[:batch].reshape(batch, 1, 6 * d)

        w = w_in[l]
        n_in = w.shape[1]
        o_q = 3 * cw
        o_gl = o_q + aw + 6 * kvw
        assert o_gl + n_gate == n_in
        n_pad = -(-n_in // MXU_TILE) * MXU_TILE
        w_pad = jnp.pad(w, ((0, 0), (0, n_pad - n_in))).astype(BF16)
        pm = _inproj(x2, mod3, norm_mix[l][None, :], w_pad, seq)

        y_conv = _conv(pm, conv_w[l], out_norm_conv[l][None, :], seq, cw)

        q_r, kc_t, vc_t, ksl, vsl, kw, vw, gates, kmax = _prep(
            pm, cosf, sinf, q_norm[l][None, :], k_norm[l], batch, seq, o_q, o_gl)

        as_bf16 = lambda ws: tuple(wi.astype(BF16) for wi in ws)
        k_cmp, v_cmp = _compress(
            kc_t.reshape(batch * N_KV, nh, CMP_STRIDE * HEAD_DIM),
            vc_t.reshape(batch * N_KV, nh, CMP_STRIDE * HEAD_DIM),
            _pe_rows(cmp_pe_k[l]), as_bf16(_w1_halves(cmp_k_w1[l])), cmp_k_b1[l][None, :],
            cmp_k_w2[l].astype(BF16),
            _pe_rows(cmp_pe_v[l]), as_bf16(_w1_halves(cmp_v_w1[l])), cmp_v_b1[l][None, :],
            cmp_v_w2[l].astype(BF16), k_norm[l], cosc, sinc)

        y_attn = _attn(q_r, k_cmp, v_cmp, ksl, vsl, kw, vw, kmax, gates,
                       out_norm_attn[l][None, :], ovt)

        x1, h2 = _outproj(y_conv, y_attn.reshape(t, aw), x2, w_out[l].astype(BF16), mod3,
                          norm_ffn[l][None, :], seq)
        x2 = _ffn(h2, ffn_w1[l].astype(BF16), ffn_w3[l].astype(BF16), ffn_w2[l].astype(BF16),
                  x1, mod3, seq)
    return x2.reshape(batch, seq, d)
```

```python
import functools

import numpy as np
import jax
import jax.numpy as jnp
from jax import lax
from jax.experimental import pallas as pl
from jax.experimental.pallas import tpu as pltpu

F32 = jnp.float32
BF16 = jnp.bfloat16

CONV_GROUP = 128
CONV_KSIZE = 3
HEAD_DIM = 128
N_KV = 2
GROUP_R = 4
N_HEADS = N_KV * GROUP_R
CMP_BLOCK = 32
CMP_STRIDE = 16
SEL_BLOCK = 64
SEL_TOP_N = 16
WINDOW = 512
ROPE_THETA = 10000.0
EPS = 1e-6
NEG_INF = -1e30
LOG2E = 1.4426950408889634
SHIFT_MARGIN = 57.0
SAFE_BOUND = 86.0

LANES = 128
VMEM_LIMIT = 56 * 1024 * 1024
FFN_VMEM_LIMIT = 60 * 1024 * 1024

ROW_TILE = 512
Q_TILE = 512
SEL_CHUNK = 512
CMP_TILE = 256
ADA_TN = 1024
INPROJ_SUB = 1
MXU_TILE = 256
FFN_TF = 512
FFN_TM = 1024
FFN_SUB = 2


def _cparams(sem, vmem_limit=VMEM_LIMIT):
    return pltpu.CompilerParams(dimension_semantics=sem, vmem_limit_bytes=vmem_limit)


def _split(a):
    hi = a.astype(BF16)
    lo = (a - hi.astype(F32)).astype(BF16)
    return hi, lo


def _dot(a, b):
    return jnp.dot(a, b, preferred_element_type=F32)


def _dot_nt(a, b):
    return lax.dot_general(a, b, (((1,), (1,)), ((), ())), preferred_element_type=F32)


def _dot3(a_hi, a_lo, b_hi, b_lo, dot=_dot):
    return (dot(a_hi, b_lo) + dot(a_lo, b_hi)) + dot(a_hi, b_hi)


def _rms(x, gain):
    return x * lax.rsqrt(jnp.mean(x * x, axis=-1, keepdims=True) + EPS) * gain


def _max_sq_norm(k):
    kf = k.astype(F32)
    return jnp.max(jnp.sum(kf * kf, axis=-1, keepdims=True))


def _rope(y, cosf, sinf):
    return y * cosf + pltpu.roll(y, HEAD_DIM // 2, 1) * sinf


def _ada_kernel(c_ref, w_ref, b_ref, o_ref):
    s_hi, s_lo = _split(jax.nn.silu(c_ref[...]))
    w_hi, w_lo = _split(w_ref[...])
    o_ref[...] = _dot3(s_hi, s_lo, w_hi, w_lo) + b_ref[...]


def _ada(c8, w, b):
    d, n = w.shape
    return pl.pallas_call(
        _ada_kernel,
        grid=(n // ADA_TN,),
        in_specs=[pl.BlockSpec((8, d), lambda j: (0, 0)),
                  pl.BlockSpec((d, ADA_TN), lambda j: (0, j)),
                  pl.BlockSpec((1, ADA_TN), lambda j: (0, j))],
        out_specs=pl.BlockSpec((8, ADA_TN), lambda j: (0, j)),
        out_shape=jax.ShapeDtypeStruct((8, n), F32),
        compiler_params=_cparams(("arbitrary",)),
        name="ada",
    )(c8, w, b)


def _inproj_kernel(x_ref, sh_ref, sc_ref, g_ref, w_ref, o_ref):
    sub = x_ref.shape[0] // INPROJ_SUB

    def norm(r):
        x = x_ref[r * sub:(r + 1) * sub, :]
        return (_rms(x, g_ref[...]) * (1.0 + sc_ref[0]) + sh_ref[0]).astype(BF16)

    h = norm(0)
    for r in range(INPROJ_SUB):
        h_next = norm(r + 1) if r + 1 < INPROJ_SUB else None
        o_ref[r * sub:(r + 1) * sub, :] = _dot(h, w_ref[...]).astype(o_ref.dtype)
        h = h_next


def _inproj(x2, mod3, gain, w, seq):
    t, d = x2.shape
    n = w.shape[1]
    tiles_per_seq = seq // ROW_TILE
    return pl.pallas_call(
        _inproj_kernel,
        grid=(t // ROW_TILE,),
        in_specs=[
            pl.BlockSpec((ROW_TILE, d), lambda i: (i, 0)),
            pl.BlockSpec((1, 1, d), lambda i: (i // tiles_per_seq, 0, 0)),
            pl.BlockSpec((1, 1, d), lambda i: (i // tiles_per_seq, 0, 1)),
            pl.BlockSpec((1, d), lambda i: (0, 0)),
            pl.BlockSpec((d, n), lambda i: (0, 0), pipeline_mode=pl.Buffered(1)),
        ],
        out_specs=pl.BlockSpec((ROW_TILE, n), lambda i: (i, 0)),
        out_shape=jax.ShapeDtypeStruct((t, n), BF16),
        compiler_params=_cparams(("arbitrary",)),
        name="inproj",
    )(x2, mod3, mod3, gain, w)


HALO = 16


def _conv_kernel(cb_ref, cc_ref, ch_ref, ccp_ref, chp_ref, w_ref, g_ref, o_ref, *, tiles_per_seq):
    i = pl.program_id(0)
    u = cc_ref[...].astype(F32) * ch_ref[...].astype(F32)
    up = ccp_ref[...].astype(F32) * chp_ref[...].astype(F32)
    up = jnp.where(i % tiles_per_seq == 0, 0.0, up)
    rows = lax.broadcasted_iota(jnp.int32, u.shape, 0)
    u1 = jnp.where(rows == 0, up[HALO - 1:HALO], pltpu.roll(u, 1, 0))
    u2 = jnp.where(rows == 0, up[HALO - 2:HALO - 1],
                   jnp.where(rows == 1, up[HALO - 1:HALO], pltpu.roll(u, 2, 0)))
    w = w_ref[...]
    y = cb_ref[...].astype(F32) * (w[2:3] * u + w[1:2] * u1 + w[0:1] * u2)
    gain = g_ref[...]
    for g in range(y.shape[1] // CONV_GROUP):
        sl = slice(g * CONV_GROUP, (g + 1) * CONV_GROUP)
        o_ref[:, sl] = _rms(y[:, sl], gain[:, sl]).astype(o_ref.dtype)


def _conv(pm, conv_w, gain, seq, cw):
    t = pm.shape[0]
    tiles_per_seq = seq // ROW_TILE
    hb = ROW_TILE // HALO

    def prev(col):
        return pl.BlockSpec((HALO, cw), lambda i: (jnp.maximum(i * hb - 1, 0), col))

    return pl.pallas_call(
        functools.partial(_conv_kernel, tiles_per_seq=tiles_per_seq),
        grid=(t // ROW_TILE,),
        in_specs=[pl.BlockSpec((ROW_TILE, cw), lambda i: (i, 0)),
                  pl.BlockSpec((ROW_TILE, cw), lambda i: (i, 1)),
                  pl.BlockSpec((ROW_TILE, cw), lambda i: (i, 2)),
                  prev(1), prev(2),
                  pl.BlockSpec((CONV_KSIZE, cw), lambda i: (0, 0)),
                  pl.BlockSpec((1, cw), lambda i: (0, 0))],
        out_specs=pl.BlockSpec((ROW_TILE, cw), lambda i: (i, 0)),
        out_shape=jax.ShapeDtypeStruct((t, cw), BF16),
        compiler_params=_cparams(("arbitrary",)),
        name="conv",
    )(pm, pm, pm, pm, pm, conv_w, gain)


def _rope_kernel(pos_ref, inv_ref, sign_ref, cos_ref, sin_ref):
    ang = pos_ref[...].astype(F32) * inv_ref[...]
    cos_ref[...] = jnp.cos(ang)
    sin_ref[...] = jnp.sin(ang) * sign_ref[...]


def _rope_tables(pos2, inv_full, sign):
    t = pos2.shape[0]
    ts = 2048
    return pl.pallas_call(
        _rope_kernel,
        grid=(t // ts,),
        in_specs=[pl.BlockSpec((ts, 1), lambda i: (i, 0)),
                  pl.BlockSpec((1, HEAD_DIM), lambda i: (0, 0)),
                  pl.BlockSpec((1, HEAD_DIM), lambda i: (0, 0))],
        out_specs=[pl.BlockSpec((ts, HEAD_DIM), lambda i: (i, 0))] * 2,
        out_shape=[jax.ShapeDtypeStruct((t, HEAD_DIM), F32)] * 2,
        compiler_params=_cparams(("arbitrary",)),
        name="rope",
    )(pos2, inv_full, sign)


def _prep_kernel(q_ref, kc_ref, vc_ref, ksl_ref, vsl_ref, kw_ref, vw_ref, gl_ref, cos_ref, sin_ref,
                 qn_ref, kn_ref,
                 q_out, kc_out, vc_out, ksl_out, vsl_out, kw_out, vw_out, gate_out, kmax_out):
    cosf = cos_ref[...]
    sinf = sin_ref[...]
    scale = HEAD_DIM ** -0.5 * LOG2E
    qn = qn_ref[...]
    kn = kn_ref[...]
    for h in range(N_HEADS):
        qh = q_ref[:, h * HEAD_DIM:(h + 1) * HEAD_DIM].astype(F32)
        q_out[0, h] = (_rope(_rms(qh, qn), cosf, sinf) * scale).astype(BF16)
    for g in range(N_KV):
        sl = slice(g * HEAD_DIM, (g + 1) * HEAD_DIM)
        kc_out[0, g] = kc_ref[:, sl]
        vc_out[0, g] = vc_ref[:, sl]
        vsl_out[0, g] = vsl_ref[:, sl]
        vw_out[0, g] = vw_ref[:, sl]
        ksl = _rope(_rms(ksl_ref[:, sl].astype(F32), kn[1:2]), cosf, sinf).astype(BF16)
        ksl_out[0, g] = ksl
        kwin = _rope(_rms(kw_ref[:, sl].astype(F32), kn[2:3]), cosf, sinf).astype(BF16)
        kw_out[0, g] = kwin
        n2 = jnp.full(kmax_out.shape[2:], jnp.maximum(_max_sq_norm(ksl), _max_sq_norm(kwin)), F32)

        @pl.when(pl.program_id(1) == 0)
        def _(g=g, n2=n2):
            kmax_out[0, g] = n2

        @pl.when(pl.program_id(1) > 0)
        def _(g=g, n2=n2):
            kmax_out[0, g] = jnp.maximum(kmax_out[0, g], n2)
    gate_out[0] = jax.nn.sigmoid(gl_ref[...].astype(F32))


def _prep(pm, cosf, sinf, q_norm, k_norm, batch, seq, q_col, gl_col):
    ts = ROW_TILE
    tiles = seq // ts
    qw = N_HEADS * HEAD_DIM
    kvw = N_KV * HEAD_DIM
    kvb = (q_col + qw) // kvw

    def rows(width, col):
        return pl.BlockSpec((ts, width), lambda b, s: (b * tiles + s, col))

    def heads(n):
        return pl.BlockSpec((1, n, ts, HEAD_DIM), lambda b, s: (b, 0, s, 0))

    kv_shape = jax.ShapeDtypeStruct((batch, N_KV, seq, HEAD_DIM), BF16)
    return pl.pallas_call(
        _prep_kernel,
        grid=(batch, tiles),
        in_specs=[rows(qw, q_col // qw)] + [rows(kvw, kvb + n) for n in range(6)] + [
                  rows(LANES, gl_col // LANES),
                  rows(HEAD_DIM, 0), rows(HEAD_DIM, 0),
                  pl.BlockSpec((1, HEAD_DIM), lambda b, s: (0, 0)),
                  pl.BlockSpec((3, HEAD_DIM), lambda b, s: (0, 0))],
        out_specs=[heads(N_HEADS)] + [heads(N_KV)] * 6 + [
                   pl.BlockSpec((1, ts, LANES), lambda b, s: (b, s, 0)),
                   pl.BlockSpec((1, N_KV, 8, LANES), lambda b, s: (b, 0, 0, 0))],
        out_shape=[jax.ShapeDtypeStruct((batch, N_HEADS, seq, HEAD_DIM), BF16)] + [kv_shape] * 6 + [
                   jax.ShapeDtypeStruct((batch, seq, LANES), F32),
                   jax.ShapeDtypeStruct((batch, N_KV, 8, LANES), F32)],
        compiler_params=_cparams(("arbitrary", "arbitrary")),
        name="prep",
    )(*([pm] * 8), cosf, sinf, q_norm, k_norm)


def _shift_up(h2, h2_next):
    n = h2.shape[0]
    rows = lax.broadcasted_iota(jnp.int32, h2.shape, 0)
    return jnp.where(rows == n - 1, h2_next[0:1], pltpu.roll(h2, n - 1, 0))


def _compress_mlp(a, a_next, pe, w1a, w1b, b1, w2):
    h1 = _dot(a, w1a)
    h2 = _shift_up(_dot(a, w1b), _dot(a_next, w1b))
    pe_term = _dot(pe, w1a)[0:1] + _dot(pe, w1b)[1:2]
    hid = jax.nn.gelu(h1 + h2 + pe_term + b1, approximate=True)
    return _dot(hid.astype(BF16), w2)


def _compress_kernel(ak_ref, akn_ref, av_ref, avn_ref,
                     pek_ref, wka, wkb, bk_ref, wk2,
                     pev_ref, wva, wvb, bv_ref, wv2,
                     kn_ref, cos_ref, sin_ref,
                     k_out, v_out, kmax_out):
    kc = _compress_mlp(ak_ref[0], akn_ref[0], pek_ref[...].astype(BF16), wka[...], wkb[...],
                       bk_ref[...], wk2[...])
    kc = _rope(_rms(kc, kn_ref[0:1]), cos_ref[0], sin_ref[0]).astype(BF16)
    k_out[0] = kc
    n2 = jnp.full(kmax_out.shape[1:], _max_sq_norm(kc), F32)

    @pl.when(pl.program_id(1) == 0)
    def _():
        kmax_out[0] = n2

    @pl.when(pl.program_id(1) > 0)
    def _():
        kmax_out[0] = jnp.maximum(kmax_out[0], n2)

    v_out[0] = _compress_mlp(av_ref[0], avn_ref[0], pev_ref[...].astype(BF16), wva[...], wvb[...],
                             bv_ref[...], wv2[...]).astype(BF16)


def _compress(ak, av, pek2, wk, bk, wk2, pev2, wv, bv, wv2, k_norm, cosc, sinc):
    bg, nh, kk = ak.shape
    hid = bk.shape[1]
    tiles = nh // CMP_TILE
    cur = pl.BlockSpec((1, CMP_TILE, kk), lambda g, t: (g, t, 0))
    per = CMP_TILE // 16
    last = nh // 16 - 1
    nxt = pl.BlockSpec((1, 16, kk), lambda g, t: (g, jnp.minimum((t + 1) * per, last), 0))
    full = lambda shp: pl.BlockSpec(shp, lambda g, t: tuple(0 for _ in shp))
    mlp = [full((8, kk)), full((kk, hid)), full((kk, hid)), full((1, hid)), full((hid, HEAD_DIM))]
    out = pl.BlockSpec((1, CMP_TILE, HEAD_DIM), lambda g, t: (g, t, 0))
    tab = pl.BlockSpec((1, CMP_TILE, HEAD_DIM), lambda g, t: (g // N_KV, t, 0))
    oshape = jax.ShapeDtypeStruct((bg, nh, HEAD_DIM), BF16)
    return pl.pallas_call(
        _compress_kernel,
        grid=(bg, tiles),
        in_specs=[cur, nxt, cur, nxt] + mlp + mlp + [full((3, HEAD_DIM)), tab, tab],
        out_specs=[out, out, pl.BlockSpec((1, 8, LANES), lambda g, t: (g, 0, 0))],
        out_shape=[oshape, oshape, jax.ShapeDtypeStruct((bg, 8, LANES), F32)],
        compiler_params=_cparams(("arbitrary", "arbitrary")),
        name="compress",
    )(ak, ak, av, av, pek2, *wk, bk, wk2, pev2, *wv, bv, wv2, k_norm, cosc, sinc)


def _count(mask):
    return jnp.sum(jnp.where(mask, 1.0, 0.0), axis=0, keepdims=True)


def _select_blocks(imp, t_pos):
    n_blocks = imp.shape[0]
    j = lax.broadcasted_iota(jnp.int32, (n_blocks, 1), 0)
    cur = jnp.right_shift(t_pos, SEL_BLOCK.bit_length() - 1)
    valid = j * SEL_BLOCK <= t_pos
    forced = (j == 0) | (j == cur) | (j == cur - 1)
    cand = valid & jnp.logical_not(forced)
    n_forced = 1.0 + jnp.where(cur >= 1, 1.0, 0.0) + jnp.where(cur >= 2, 1.0, 0.0)
    k = float(SEL_TOP_N) - n_forced

    imp_c = jnp.where(cand, imp, -1.0)

    def value_step(it, tb):
        trial = tb | jnp.left_shift(jnp.int32(1), 30 - it)
        keep = _count(imp_c >= pltpu.bitcast(trial, F32)) >= k
        return jnp.where(keep, trial, tb)

    tb = lax.fori_loop(0, 31, value_step, jnp.zeros(t_pos.shape, jnp.int32))
    thr = pltpu.bitcast(tb, F32)
    above = imp_c > thr
    need = k - _count(above)

    j_tied = jnp.where(imp_c == thr, j, 2 * n_blocks)

    def index_step(it, jb):
        trial = jb | jnp.left_shift(jnp.int32(1), (n_blocks.bit_length() - 1) - it)
        keep = _count(j_tied < trial) < need
        return jnp.where(keep, trial, jb)

    def index_search():
        return lax.fori_loop(0, n_blocks.bit_length(), index_step,
                             jnp.zeros(t_pos.shape, jnp.int32))

    all_tied_taken = jnp.min(need - _count(j_tied < 2 * n_blocks)) >= 0.0
    jb = lax.cond(all_tied_taken,
                  lambda: jnp.full(t_pos.shape, 2 * n_blocks - 1, jnp.int32), index_search)
    return jnp.where((forced & valid) | above | (j_tied <= jb), 1.0, 0.0)


def _attn_kernel(q_ref, kc_ref, vc_ref, ksl_ref, vsl_ref, kw_ref, vw_ref, kmax_ref, kcmax_ref,
                 gate_ref, gn_ref, ovt_ref, o_ref, qa_ref, m_ref, l_ref, acc_ref, oslc_ref, s_ref,
                 ocmp_ref, impt_ref, owin_ref):
    gi = pl.program_id(1)
    i = pl.program_id(2)
    tq = q_ref.shape[2]
    rows = GROUP_R * tq
    nc = kc_ref.shape[1]
    nsp = ovt_ref.shape[0]
    ck = SEL_CHUNK
    blocks_per_chunk = ck // SEL_BLOCK

    q_hi = q_ref[0].reshape(rows, HEAD_DIM)
    t_row = i * tq + jnp.bitwise_and(lax.broadcasted_iota(jnp.int32, (rows, 1), 0), tq - 1)

    qf = q_hi.astype(F32)
    k2max = jnp.maximum(kmax_ref[0, 0][0:1, 0:1], kcmax_ref[0][0:1, 0:1])
    bound = jnp.sqrt(jnp.sum(qf * qf, axis=-1, keepdims=True) * k2max) * 1.01
    safe = jnp.max(bound) <= SAFE_BOUND
    unsafe = jnp.logical_not(safe)
    shift = SHIFT_MARGIN - bound

    def importance(pn, ncols):
        psum = pn(0)
        for r in range(1, GROUP_R):
            psum = psum + pn(r)
        impt_ref[...] = _dot_nt(ovt_ref[:, 0:ncols], psum.astype(BF16))

    def visible(ncols):
        n_end = lax.broadcasted_iota(jnp.int32, (1, ncols), 1) * CMP_STRIDE + (CMP_BLOCK - 1)
        return n_end <= t_row

    def compressed_fixed_shift(ncols):
        s = _dot_nt(q_hi, kc_ref[0, 0:ncols, :])
        p = jnp.where(visible(ncols), jnp.exp2(s + shift), 0.0)
        l = jnp.sum(p, axis=-1, keepdims=True)
        some = l > 0.0
        inv = jnp.where(some, 1.0 / jnp.where(some, l, 1.0), 0.0)
        ocmp_ref[...] = _dot(p.astype(BF16), vc_ref[0, 0:ncols, :]) * inv
        importance(lambda r: p[r * tq:(r + 1) * tq] * inv[r * tq:(r + 1) * tq], ncols)

    first_half = 2 * i < pl.num_programs(2)

    @pl.when(safe & first_half)
    def _():
        compressed_fixed_shift(nc // 2)

    @pl.when(safe & jnp.logical_not(first_half))
    def _():
        compressed_fixed_shift(nc)

    t_q = i * tq + lax.broadcasted_iota(jnp.int32, (tq, 1), 0)

    def head_rows(r):
        return pl.ds(pl.multiple_of(r * tq, tq), tq)

    @pl.when(unsafe)
    def _():
        impt_ref[...] = jnp.zeros(impt_ref.shape, F32)
        n_end = lax.broadcasted_iota(jnp.int32, (1, nc), 1) * CMP_STRIDE + (CMP_BLOCK - 1)

        def head(r, carry):
            s = jnp.where(n_end <= t_q, _dot_nt(q_ref[0, r], kc_ref[0]), NEG_INF)
            m = jnp.max(s, axis=-1, keepdims=True)
            p = jnp.exp2(s - m)
            l = jnp.sum(p, axis=-1, keepdims=True)
            pc = (p * jnp.where(m > 0.5 * NEG_INF, 1.0 / l, 0.0)).astype(BF16)
            ocmp_ref[head_rows(r), :] = _dot(pc, vc_ref[0])
            impt_ref[...] += _dot_nt(ovt_ref[...], pc)
            return carry

        lax.fori_loop(0, GROUP_R, head, 0)

    o_cmp = ocmp_ref[...]

    t_pos = i * tq + lax.broadcasted_iota(jnp.int32, (1, tq), 1)
    sel = jnp.transpose(_select_blocks(impt_ref[...], t_pos))
    sel = jnp.concatenate([sel] * GROUP_R, axis=0) > 0.5

    def augmented_keys(c):
        k0 = pl.multiple_of(c * ck, ck)
        kblk = c * blocks_per_chunk + jnp.right_shift(
            lax.broadcasted_iota(jnp.int32, (ck, 1), 0), SEL_BLOCK.bit_length() - 1)
        lane = lax.broadcasted_iota(jnp.int32, (1, LANES), 1)
        onehot = jnp.where(jnp.bitwise_and(kblk, LANES - 1) == lane, 1.0, 0.0).astype(BF16)
        return k0, jnp.concatenate([ksl_ref[0, 0, pl.ds(k0, ck), :], onehot], axis=1)

    def scores(c):
        _, ka = augmented_keys(c)
        sc = _dot_nt(qa_ref[c // (LANES // blocks_per_chunk)], ka)
        kp = c * ck + lax.broadcasted_iota(jnp.int32, (1, ck), 1)
        return jnp.where(kp <= t_row, sc, NEG_INF)

    def set_lane_values(selected_value):
        lanev = jnp.where(sel, selected_value, NEG_INF).astype(BF16)
        for w in range(nsp // LANES):
            qa_ref[w] = jnp.concatenate([q_hi, lanev[:, w * LANES:(w + 1) * LANES]], axis=1)

    c_last = (i * tq) // ck

    wk = WINDOW + tq
    start = pl.multiple_of(jnp.maximum(i * tq - WINDOW, 0), tq)

    def window_fixed_shift():
        sw = _dot_nt(q_hi, kw_ref[0, 0, pl.ds(start, wk), :]) + shift
        parts = []
        for b in range(wk // tq):
            kp = start + b * tq + lax.broadcasted_iota(jnp.int32, (1, tq), 1)
            ok = kp <= t_row
            if b == 0:
                ok = ok & (kp > t_row - WINDOW)
            parts.append(jnp.where(ok, jnp.exp2(sw[:, b * tq:(b + 1) * tq]), 0.0).astype(BF16))
        va = jnp.concatenate([vw_ref[0, 0, pl.ds(start, wk), :], jnp.ones((wk, HEAD_DIM), BF16)], axis=1)
        acc = _dot(jnp.concatenate(parts, axis=1), va)
        owin_ref[...] = acc[:, 0:HEAD_DIM] / acc[:, HEAD_DIM:HEAD_DIM + 1]

    @pl.when(safe)
    def _():
        window_fixed_shift()
        set_lane_values(shift)
        acc_ref[...] = jnp.zeros(acc_ref.shape, F32)
        ones = jnp.ones((ck, HEAD_DIM), BF16)

        def produce(c, slot):
            s_ref[slot] = scores(jnp.minimum(c, c_last))

        def consume(c, slot):
            k0 = pl.multiple_of(c * ck, ck)
            va = jnp.concatenate([vsl_ref[0, 0, pl.ds(k0, ck), :], ones], axis=1)
            acc_ref[...] += _dot(jnp.exp2(s_ref[slot]).astype(BF16), va)

        def pair(kk, carry):
            c = 2 * kk
            produce(c + 1, 1)
            consume(c, 0)
            produce(c + 2, 0)
            consume(c + 1, 1)
            return carry

        n_chunks = c_last + 1
        produce(0, 0)
        lax.fori_loop(0, n_chunks // 2, pair, 0)

        @pl.when(n_chunks % 2 == 1)
        def _():
            consume(c_last, 0)

        oslc_ref[...] = acc_ref[:, 0:HEAD_DIM] / acc_ref[:, HEAD_DIM:HEAD_DIM + 1]

    @pl.when(unsafe)
    def _():
        set_lane_values(0.0)
        kp_w = start + lax.broadcasted_iota(jnp.int32, (1, wk), 1)

        def head(r, carry):
            rs = head_rows(r)
            sw = _dot_nt(q_ref[0, r], kw_ref[0, 0, pl.ds(start, wk), :])
            sw = jnp.where((kp_w <= t_q) & (kp_w > t_q - WINDOW), sw, NEG_INF)
            pw = jnp.exp2(sw - jnp.max(sw, axis=-1, keepdims=True))
            owin_ref[rs, :] = (_dot(pw.astype(BF16), vw_ref[0, 0, pl.ds(start, wk), :])
                               / jnp.sum(pw, axis=-1, keepdims=True))

            m_ref[rs, :] = jnp.full((tq, 1), NEG_INF, F32)
            l_ref[rs, :] = jnp.zeros((tq, 1), F32)
            acc_ref[rs, 0:HEAD_DIM] = jnp.zeros((tq, HEAD_DIM), F32)

            def chunk(c, carry2):
                k0, ka = augmented_keys(c)
                sc = _dot_nt(qa_ref[c // (LANES // blocks_per_chunk), rs, :], ka)
                kp = c * ck + lax.broadcasted_iota(jnp.int32, (1, ck), 1)
                sc = jnp.where(kp <= t_q, sc, NEG_INF)
                m_old = m_ref[rs, :]
                m_new = jnp.maximum(m_old, jnp.max(sc, axis=-1, keepdims=True))
                alpha = jnp.exp2(m_old - m_new)
                pe = jnp.exp2(sc - m_new)
                l_ref[rs, :] = alpha * l_ref[rs, :] + jnp.sum(pe, axis=-1, keepdims=True)
                acc_ref[rs, 0:HEAD_DIM] = alpha * acc_ref[rs, 0:HEAD_DIM] + _dot(
                    pe.astype(BF16), vsl_ref[0, 0, pl.ds(k0, ck), :])
                m_ref[rs, :] = m_new
                return carry2

            lax.fori_loop(0, c_last + 1, chunk, 0)
            oslc_ref[rs, :] = acc_ref[rs, 0:HEAD_DIM] / l_ref[rs, :]
            return carry

        lax.fori_loop(0, GROUP_R, head, 0)

    o_slc = oslc_ref[...]
    o_win = owin_ref[...]

    gates = gate_ref[0]
    gn = gn_ref[...]
    n_gate = 3 * GROUP_R
    for r in range(GROUP_R):
        sl = slice(r * tq, (r + 1) * tq)

        def gate(branch, r=r):
            col = 3 * r + branch
            return jnp.where(gi == 0, gates[:, col:col + 1], gates[:, n_gate + col:n_gate + col + 1])

        o = gate(0) * o_cmp[sl] + gate(1) * o_slc[sl] + gate(2) * o_win[sl]
        hs = slice(r * HEAD_DIM, (r + 1) * HEAD_DIM)
        o_ref[0, :, hs] = _rms(o, gn[:, hs]).astype(o_ref.dtype)


def _attn(q_r, k_cmp, v_cmp, ksl, vsl, kw, vw, kmax, kcmax, gates, gain, ovt):
    batch, _, seq, _ = q_r.shape
    nc = k_cmp.shape[1]
    nsp = ovt.shape[0]
    tq = Q_TILE
    rows = GROUP_R * tq
    gw = GROUP_R * HEAD_DIM
    cmp_spec = pl.BlockSpec((1, nc, HEAD_DIM), lambda b, g, i: (b * N_KV + g, 0, 0))
    kv_spec = pl.BlockSpec((1, 1, seq, HEAD_DIM), lambda b, g, i: (b, g, 0, 0),
                           pipeline_mode=pl.Buffered(1))
    return pl.pallas_call(
        _attn_kernel,
        grid=(batch, N_KV, seq // tq),
        in_specs=[pl.BlockSpec((1, GROUP_R, tq, HEAD_DIM), lambda b, g, i: (b, g, i, 0)),
                  cmp_spec, cmp_spec,
                  kv_spec, kv_spec, kv_spec, kv_spec,
                  pl.BlockSpec((1, 1, 8, LANES), lambda b, g, i: (b, g, 0, 0)),
                  pl.BlockSpec((1, 8, LANES), lambda b, g, i: (b * N_KV + g, 0, 0)),
                  pl.BlockSpec((1, tq, LANES), lambda b, g, i: (b, i, 0)),
                  pl.BlockSpec((1, gw), lambda b, g, i: (0, g)),
                  pl.BlockSpec((nsp, nc), lambda b, g, i: (0, 0))],
        out_specs=pl.BlockSpec((1, tq, gw), lambda b, g, i: (b, i, g)),
        out_shape=jax.ShapeDtypeStruct((batch, seq, N_KV * gw), BF16),
        scratch_shapes=[pltpu.VMEM((nsp // LANES, rows, 2 * HEAD_DIM), BF16),
                        pltpu.VMEM((rows, 1), F32), pltpu.VMEM((rows, 1), F32),
                        pltpu.VMEM((rows, 2 * HEAD_DIM), F32),
                        pltpu.VMEM((rows, HEAD_DIM), F32),
                        pltpu.VMEM((2, rows, SEL_CHUNK), F32),
                        pltpu.VMEM((rows, HEAD_DIM), F32),
                        pltpu.VMEM((nsp, tq), F32),
                        pltpu.VMEM((rows, HEAD_DIM), F32)],
        compiler_params=_cparams(("arbitrary", "arbitrary", "arbitrary")),
        name="attn",
    )(q_r, k_cmp, v_cmp, ksl, vsl, kw, vw, kmax, kcmax, gates, gain, ovt)


def _outproj_kernel(yc_ref, ya_ref, x_ref, w_ref, ga_ref, g_ref, sh_ref, sc_ref, x1_ref, h2_ref):
    cw = yc_ref.shape[1]
    acc = _dot(yc_ref[...], w_ref[0:cw, :]) + _dot(ya_ref[...], w_ref[cw:, :])
    x1 = x_ref[...] + ga_ref[0] * acc
    x1_ref[...] = x1
    h2_ref[...] = (_rms(x1, g_ref[...]) * (1.0 + sc_ref[0]) + sh_ref[0]).astype(h2_ref.dtype)


def _outproj(y_conv, y_attn, x2, w_out, mod3, gain, seq):
    t, d = x2.shape
    tiles_per_seq = seq // ROW_TILE
    mod = lambda k: pl.BlockSpec((1, 1, d), lambda i: (i // tiles_per_seq, 0, k))
    row = lambda w: pl.BlockSpec((ROW_TILE, w), lambda i: (i, 0))
    return pl.pallas_call(
        _outproj_kernel,
        grid=(t // ROW_TILE,),
        in_specs=[row(y_conv.shape[1]), row(y_attn.shape[1]), row(d),
                  pl.BlockSpec(w_out.shape, lambda i: (0, 0)),
                  mod(2), pl.BlockSpec((1, d), lambda i: (0, 0)), mod(3), mod(4)],
        out_specs=[row(d), row(d)],
        out_shape=[jax.ShapeDtypeStruct((t, d), F32), jax.ShapeDtypeStruct((t, d), BF16)],
        compiler_params=_cparams(("arbitrary",)),
        name="outproj",
    )(y_conv, y_attn, x2, w_out, mod3, gain, mod3, mod3)


def _ffn_kernel(h_ref, w1_ref, w3_ref, w2_ref, x1_ref, gf_ref, o_ref):
    f = pl.program_id(1)
    last = pl.num_programs(1) - 1
    sub = h_ref.shape[0] // FFN_SUB

    def up(r):
        h = h_ref[r * sub:(r + 1) * sub, :]
        return _dot(h, w1_ref[...]), _dot(h, w3_ref[...])

    def down(r, ab):
        rs = slice(r * sub, (r + 1) * sub)
        o_ref[rs, :] += _dot((jax.nn.silu(ab[0]) * ab[1]).astype(BF16), w2_ref[...])

    @pl.when(f == 0)
    def _():
        o_ref[...] = jnp.zeros(o_ref.shape, F32)

    ab = up(0)
    for r in range(FFN_SUB):
        ab_next = up(r + 1) if r + 1 < FFN_SUB else None
        down(r, ab)
        ab = ab_next

    @pl.when(f == last)
    def _():
        o_ref[...] = x1_ref[...] + gf_ref[0] * o_ref[...]


def _ffn(h2, w1, w3, w2, x1, mod3, seq):
    t, d = x1.shape
    dff = w1.shape[1]
    tm = FFN_TM
    tiles_per_seq = seq // tm
    return pl.pallas_call(
        _ffn_kernel,
        grid=(t // tm, dff // FFN_TF),
        in_specs=[pl.BlockSpec((tm, d), lambda i, f: (i, 0)),
                  pl.BlockSpec((d, FFN_TF), lambda i, f: (0, f)),
                  pl.BlockSpec((d, FFN_TF), lambda i, f: (0, f)),
                  pl.BlockSpec((FFN_TF, d), lambda i, f: (f, 0)),
                  pl.BlockSpec((tm, d), lambda i, f: (i, 0)),
                  pl.BlockSpec((1, 1, d), lambda i, f: (i // tiles_per_seq, 0, 5))],
        out_specs=pl.BlockSpec((tm, d), lambda i, f: (i, 0)),
        out_shape=jax.ShapeDtypeStruct((t, d), F32),
        compiler_params=_cparams(("arbitrary", "arbitrary"), FFN_VMEM_LIMIT),
        name="ffn",
    )(h2, w1, w3, w2, x1, mod3)


def _overlap_matrix(n_cmp_pad, n_sel, n_sel_pad):
    cmp_start = np.arange(n_cmp_pad) * CMP_STRIDE
    cmp_end = cmp_start + CMP_BLOCK - 1
    sel_start = np.arange(n_sel) * SEL_BLOCK
    ov = ((cmp_start[None, :] < sel_start[:, None] + SEL_BLOCK)
          & (cmp_end[None, :] >= sel_start[:, None])).astype(np.float32)
    return np.pad(ov, ((0, n_sel_pad - n_sel), (0, 0)))


def _w1_halves(w1):
    l, dk, hid = w1.shape
    half = l // 2
    return w1[:half].reshape(half * dk, hid), w1[half:].reshape(half * dk, hid)


def _pe_rows(pe):
    l, dk = pe.shape
    half = l // 2
    rows = jnp.stack([pe[:half].reshape(half * dk), pe[half:].reshape(half * dk)])
    return jnp.pad(rows, ((0, 6), (0, 0)))


def kernel(x, c, positions, ada_w, ada_b, norm_mix, norm_ffn, w_in, conv_w, cmp_pe_k, cmp_k_w1, cmp_k_b1, cmp_k_w2, cmp_pe_v, cmp_v_w1, cmp_v_b1, cmp_v_w2, q_norm, k_norm, out_norm_conv, out_norm_attn, w_out, ffn_w1, ffn_w3, ffn_w2):
    batch, seq, d = x.shape
    t = batch * seq
    depth = ada_w.shape[0]
    cw = conv_w.shape[2]
    aw = N_HEADS * HEAD_DIM
    kvw = N_KV * HEAD_DIM
    n_gate = 3 * N_HEADS
    assert seq % max(ROW_TILE, SEL_CHUNK, FFN_TM) == 0 and seq >= WINDOW + Q_TILE
    assert seq // SEL_BLOCK >= SEL_TOP_N and CMP_BLOCK == 2 * CMP_STRIDE
    nh = seq // CMP_STRIDE
    n_sel = seq // SEL_BLOCK
    n_sel_pad = -(-n_sel // LANES) * LANES
    n_sel_pad = 1 << (n_sel_pad - 1).bit_length()
    assert nh % CMP_TILE == 0

    inv = 1.0 / (ROPE_THETA ** (jnp.arange(0, HEAD_DIM, 2, dtype=F32) / HEAD_DIM))
    inv_full = jnp.concatenate([inv, inv])[None, :]
    sign = jnp.concatenate([-jnp.ones(HEAD_DIM // 2, F32), jnp.ones(HEAD_DIM // 2, F32)])[None, :]
    cosf, sinf = _rope_tables(positions.reshape(t, 1), inv_full, sign)
    at_end = lambda tab: jnp.pad(
        tab.reshape(batch, nh, CMP_STRIDE, HEAD_DIM)[:, 1:, CMP_STRIDE - 1], ((0, 0), (0, 1), (0, 0)))
    cosc, sinc = at_end(cosf), at_end(sinf)
    ovt = jnp.asarray(_overlap_matrix(nh, n_sel, n_sel_pad), BF16)

    c8 = jnp.pad(c, ((0, 8 - batch), (0, 0)))
    x2 = x.reshape(t, d)
    for l in range(depth):
        mod = _ada(c8, ada_w[l], ada_b[l][None, :])
        mod3 = mod[:batch].reshape(batch, 1, 6 * d)

        w = w_in[l]
        n_in = w.shape[1]
        o_q = 3 * cw
        o_gl = o_q + aw + 6 * kvw
        assert o_gl + n_gate == n_in
        n_pad = -(-n_in // MXU_TILE) * MXU_TILE
        w_pad = jnp.pad(w, ((0, 0), (0, n_pad - n_in))).astype(BF16)
        pm = _inproj(x2, mod3, norm_mix[l][None, :], w_pad, seq)

        y_conv = _conv(pm, conv_w[l], out_norm_conv[l][None, :], seq, cw)

        q_r, kc_t, vc_t, ksl, vsl, kw, vw, gates, kmax = _prep(
            pm, cosf, sinf, q_norm[l][None, :], k_norm[l], batch, seq, o_q, o_gl)

        as_bf16 = lambda ws: tuple(wi.astype(BF16) for wi in ws)
        k_cmp, v_cmp, kcmax = _compress(
            kc_t.reshape(batch * N_KV, nh, CMP_STRIDE * HEAD_DIM),
            vc_t.reshape(batch * N_KV, nh, CMP_STRIDE * HEAD_DIM),
            _pe_rows(cmp_pe_k[l]), as_bf16(_w1_halves(cmp_k_w1[l])), cmp_k_b1[l][None, :],
            cmp_k_w2[l].astype(BF16),
            _pe_rows(cmp_pe_v[l]), as_bf16(_w1_halves(cmp_v_w1[l])), cmp_v_b1[l][None, :],
            cmp_v_w2[l].astype(BF16), k_norm[l], cosc, sinc)

        y_attn = _attn(q_r, k_cmp, v_cmp, ksl, vsl, kw, vw, kmax, kcmax, gates,
                       out_norm_attn[l][None, :], ovt)

        x1, h2 = _outproj(y_conv, y_attn.reshape(t, aw), x2, w_out[l].astype(BF16), mod3,
                          norm_ffn[l][None, :], seq)
        x2 = _ffn(h2, ffn_w1[l].astype(BF16), ffn_w3[l].astype(BF16), ffn_w2[l].astype(BF16),
                  x1, mod3, seq)
    return x2.reshape(batch, seq, d)
```

```python
import functools

import numpy as np
import jax
import jax.numpy as jnp
from jax import lax
from jax.experimental import pallas as pl
from jax.experimental.pallas import tpu as pltpu

F32 = jnp.float32
BF16 = jnp.bfloat16

CONV_GROUP = 128
CONV_KSIZE = 3
HEAD_DIM = 128
N_KV = 2
GROUP_R = 4
N_HEADS = N_KV * GROUP_R
CMP_BLOCK = 32
CMP_STRIDE = 16
SEL_BLOCK = 64
SEL_TOP_N = 16
WINDOW = 512
ROPE_THETA = 10000.0
EPS = 1e-6
NEG_INF = -1e30
LOG2E = 1.4426950408889634
SHIFT_MARGIN = 57.0
SAFE_BOUND = 86.0

LANES = 128
VMEM_LIMIT = 56 * 1024 * 1024
FFN_VMEM_LIMIT = 60 * 1024 * 1024

ROW_TILE = 512
Q_TILE = 512
SEL_CHUNK = 512
CMP_TILE = 256
ADA_TN = 1024
INPROJ_SUB = 1
MXU_TILE = 256
FFN_TF = 512
FFN_TM = 1024
FFN_SUB = 2


def _cparams(sem, vmem_limit=VMEM_LIMIT):
    return pltpu.CompilerParams(dimension_semantics=sem, vmem_limit_bytes=vmem_limit)


def _split(a):
    hi = a.astype(BF16)
    lo = (a - hi.astype(F32)).astype(BF16)
    return hi, lo


def _dot(a, b):
    return jnp.dot(a, b, preferred_element_type=F32)


def _dot_nt(a, b):
    return lax.dot_general(a, b, (((1,), (1,)), ((), ())), preferred_element_type=F32)


def _dot3(a_hi, a_lo, b_hi, b_lo, dot=_dot):
    return (dot(a_hi, b_lo) + dot(a_lo, b_hi)) + dot(a_hi, b_hi)


def _rms(x, gain):
    return x * lax.rsqrt(jnp.mean(x * x, axis=-1, keepdims=True) + EPS) * gain


def _max_sq_norm(k):
    kf = k.astype(F32)
    return jnp.max(jnp.sum(kf * kf, axis=-1, keepdims=True))


def _rope(y, cosf, sinf):
    return y * cosf + pltpu.roll(y, HEAD_DIM // 2, 1) * sinf


def _ada_kernel(c_ref, w_ref, b_ref, o_ref):
    s_hi, s_lo = _split(jax.nn.silu(c_ref[...]))
    w_hi, w_lo = _split(w_ref[...])
    o_ref[...] = _dot3(s_hi, s_lo, w_hi, w_lo) + b_ref[...]


def _ada(c8, w, b):
    d, n = w.shape
    return pl.pallas_call(
        _ada_kernel,
        grid=(n // ADA_TN,),
        in_specs=[pl.BlockSpec((8, d), lambda j: (0, 0)),
                  pl.BlockSpec((d, ADA_TN), lambda j: (0, j)),
                  pl.BlockSpec((1, ADA_TN), lambda j: (0, j))],
        out_specs=pl.BlockSpec((8, ADA_TN), lambda j: (0, j)),
        out_shape=jax.ShapeDtypeStruct((8, n), F32),
        compiler_params=_cparams(("arbitrary",)),
        name="ada",
    )(c8, w, b)


def _inproj_kernel(x_ref, sh_ref, sc_ref, g_ref, w_ref, o_ref):
    sub = x_ref.shape[0] // INPROJ_SUB

    def norm(r):
        x = x_ref[r * sub:(r + 1) * sub, :]
        return (_rms(x, g_ref[...]) * (1.0 + sc_ref[0]) + sh_ref[0]).astype(BF16)

    h = norm(0)
    for r in range(INPROJ_SUB):
        h_next = norm(r + 1) if r + 1 < INPROJ_SUB else None
        o_ref[r * sub:(r + 1) * sub, :] = _dot(h, w_ref[...]).astype(o_ref.dtype)
        h = h_next


def _inproj(x2, mod3, gain, w, seq):
    t, d = x2.shape
    n = w.shape[1]
    tiles_per_seq = seq // ROW_TILE
    return pl.pallas_call(
        _inproj_kernel,
        grid=(t // ROW_TILE,),
        in_specs=[
            pl.BlockSpec((ROW_TILE, d), lambda i: (i, 0)),
            pl.BlockSpec((1, 1, d), lambda i: (i // tiles_per_seq, 0, 0)),
            pl.BlockSpec((1, 1, d), lambda i: (i // tiles_per_seq, 0, 1)),
            pl.BlockSpec((1, d), lambda i: (0, 0)),
            pl.BlockSpec((d, n), lambda i: (0, 0), pipeline_mode=pl.Buffered(1)),
        ],
        out_specs=pl.BlockSpec((ROW_TILE, n), lambda i: (i, 0)),
        out_shape=jax.ShapeDtypeStruct((t, n), BF16),
        compiler_params=_cparams(("arbitrary",)),
        name="inproj",
    )(x2, mod3, mod3, gain, w)


HALO = 16


def _conv_kernel(cb_ref, cc_ref, ch_ref, ccp_ref, chp_ref, w_ref, g_ref, o_ref, *, tiles_per_seq):
    i = pl.program_id(0)
    u = cc_ref[...].astype(F32) * ch_ref[...].astype(F32)
    up = ccp_ref[...].astype(F32) * chp_ref[...].astype(F32)
    up = jnp.where(i % tiles_per_seq == 0, 0.0, up)
    rows = lax.broadcasted_iota(jnp.int32, u.shape, 0)
    u1 = jnp.where(rows == 0, up[HALO - 1:HALO], pltpu.roll(u, 1, 0))
    u2 = jnp.where(rows == 0, up[HALO - 2:HALO - 1],
                   jnp.where(rows == 1, up[HALO - 1:HALO], pltpu.roll(u, 2, 0)))
    w = w_ref[...]
    y = cb_ref[...].astype(F32) * (w[2:3] * u + w[1:2] * u1 + w[0:1] * u2)
    gain = g_ref[...]
    for g in range(y.shape[1] // CONV_GROUP):
        sl = slice(g * CONV_GROUP, (g + 1) * CONV_GROUP)
        o_ref[:, sl] = _rms(y[:, sl], gain[:, sl]).astype(o_ref.dtype)


def _conv(pm, conv_w, gain, seq, cw):
    t = pm.shape[0]
    tiles_per_seq = seq // ROW_TILE
    hb = ROW_TILE // HALO

    def prev(col):
        return pl.BlockSpec((HALO, cw), lambda i: (jnp.maximum(i * hb - 1, 0), col))

    return pl.pallas_call(
        functools.partial(_conv_kernel, tiles_per_seq=tiles_per_seq),
        grid=(t // ROW_TILE,),
        in_specs=[pl.BlockSpec((ROW_TILE, cw), lambda i: (i, 0)),
                  pl.BlockSpec((ROW_TILE, cw), lambda i: (i, 1)),
                  pl.BlockSpec((ROW_TILE, cw), lambda i: (i, 2)),
                  prev(1), prev(2),
                  pl.BlockSpec((CONV_KSIZE, cw), lambda i: (0, 0)),
                  pl.BlockSpec((1, cw), lambda i: (0, 0))],
        out_specs=pl.BlockSpec((ROW_TILE, cw), lambda i: (i, 0)),
        out_shape=jax.ShapeDtypeStruct((t, cw), BF16),
        compiler_params=_cparams(("arbitrary",)),
        name="conv",
    )(pm, pm, pm, pm, pm, conv_w, gain)


def _rope_kernel(pos_ref, inv_ref, sign_ref, cos_ref, sin_ref):
    ang = pos_ref[...].astype(F32) * inv_ref[...]
    cos_ref[...] = jnp.cos(ang)
    sin_ref[...] = jnp.sin(ang) * sign_ref[...]


def _rope_tables(pos2, inv_full, sign):
    t = pos2.shape[0]
    ts = 2048
    return pl.pallas_call(
        _rope_kernel,
        grid=(t // ts,),
        in_specs=[pl.BlockSpec((ts, 1), lambda i: (i, 0)),
                  pl.BlockSpec((1, HEAD_DIM), lambda i: (0, 0)),
                  pl.BlockSpec((1, HEAD_DIM), lambda i: (0, 0))],
        out_specs=[pl.BlockSpec((ts, HEAD_DIM), lambda i: (i, 0))] * 2,
        out_shape=[jax.ShapeDtypeStruct((t, HEAD_DIM), F32)] * 2,
        compiler_params=_cparams(("arbitrary",)),
        name="rope",
    )(pos2, inv_full, sign)


def _prep_kernel(q_ref, kc_ref, vc_ref, ksl_ref, vsl_ref, kw_ref, vw_ref, gl_ref, cos_ref, sin_ref,
                 qn_ref, kn_ref,
                 q_out, kc_out, vc_out, ksl_out, vsl_out, kw_out, vw_out, gate_out, kmax_out):
    cosf = cos_ref[...]
    sinf = sin_ref[...]
    scale = HEAD_DIM ** -0.5 * LOG2E
    qn = qn_ref[...]
    kn = kn_ref[...]
    for h in range(N_HEADS):
        qh = q_ref[:, h * HEAD_DIM:(h + 1) * HEAD_DIM].astype(F32)
        q_out[0, h] = (_rope(_rms(qh, qn), cosf, sinf) * scale).astype(BF16)
    for g in range(N_KV):
        sl = slice(g * HEAD_DIM, (g + 1) * HEAD_DIM)
        kc_out[0, g] = kc_ref[:, sl]
        vc_out[0, g] = vc_ref[:, sl]
        vsl_out[0, g] = vsl_ref[:, sl]
        vw_out[0, g] = vw_ref[:, sl]
        ksl = _rope(_rms(ksl_ref[:, sl].astype(F32), kn[1:2]), cosf, sinf).astype(BF16)
        ksl_out[0, g] = ksl
        kwin = _rope(_rms(kw_ref[:, sl].astype(F32), kn[2:3]), cosf, sinf).astype(BF16)
        kw_out[0, g] = kwin
        n2 = jnp.full(kmax_out.shape[2:], jnp.maximum(_max_sq_norm(ksl), _max_sq_norm(kwin)), F32)

        @pl.when(pl.program_id(1) == 0)
        def _(g=g, n2=n2):
            kmax_out[0, g] = n2

        @pl.when(pl.program_id(1) > 0)
        def _(g=g, n2=n2):
            kmax_out[0, g] = jnp.maximum(kmax_out[0, g], n2)
    gate_out[0] = jax.nn.sigmoid(gl_ref[...].astype(F32))


def _prep(pm, cosf, sinf, q_norm, k_norm, batch, seq, q_col, gl_col):
    ts = ROW_TILE
    tiles = seq // ts
    qw = N_HEADS * HEAD_DIM
    kvw = N_KV * HEAD_DIM
    kvb = (q_col + qw) // kvw

    def rows(width, col):
        return pl.BlockSpec((ts, width), lambda b, s: (b * tiles + s, col))

    def heads(n):
        return pl.BlockSpec((1, n, ts, HEAD_DIM), lambda b, s: (b, 0, s, 0))

    kv_shape = jax.ShapeDtypeStruct((batch, N_KV, seq, HEAD_DIM), BF16)
    return pl.pallas_call(
        _prep_kernel,
        grid=(batch, tiles),
        in_specs=[rows(qw, q_col // qw)] + [rows(kvw, kvb + n) for n in range(6)] + [
                  rows(LANES, gl_col // LANES),
                  rows(HEAD_DIM, 0), rows(HEAD_DIM, 0),
                  pl.BlockSpec((1, HEAD_DIM), lambda b, s: (0, 0)),
                  pl.BlockSpec((3, HEAD_DIM), lambda b, s: (0, 0))],
        out_specs=[heads(N_HEADS)] + [heads(N_KV)] * 6 + [
                   pl.BlockSpec((1, ts, LANES), lambda b, s: (b, s, 0)),
                   pl.BlockSpec((1, N_KV, 8, LANES), lambda b, s: (b, 0, 0, 0))],
        out_shape=[jax.ShapeDtypeStruct((batch, N_HEADS, seq, HEAD_DIM), BF16)] + [kv_shape] * 6 + [
                   jax.ShapeDtypeStruct((batch, seq, LANES), F32),
                   jax.ShapeDtypeStruct((batch, N_KV, 8, LANES), F32)],
        compiler_params=_cparams(("arbitrary", "arbitrary")),
        name="prep",
    )(*([pm] * 8), cosf, sinf, q_norm, k_norm)


def _shift_up(h2, h2_next):
    n = h2.shape[0]
    rows = lax.broadcasted_iota(jnp.int32, h2.shape, 0)
    return jnp.where(rows == n - 1, h2_next[0:1], pltpu.roll(h2, n - 1, 0))


def _compress_mlp(a, a_next, pe, w1a, w1b, b1, w2):
    h1 = _dot(a, w1a)
    h2 = _shift_up(_dot(a, w1b), _dot(a_next, w1b))
    pe_term = _dot(pe, w1a)[0:1] + _dot(pe, w1b)[1:2]
    hid = jax.nn.gelu(h1 + h2 + pe_term + b1, approximate=True)
    return _dot(hid.astype(BF16), w2)


def _compress_kernel(ak_ref, akn_ref, av_ref, avn_ref,
                     pek_ref, wka, wkb, bk_ref, wk2,
                     pev_ref, wva, wvb, bv_ref, wv2,
                     kn_ref, cos_ref, sin_ref,
                     k_out, v_out, kmax_out):
    kc = _compress_mlp(ak_ref[0], akn_ref[0], pek_ref[...].astype(BF16), wka[...], wkb[...],
                       bk_ref[...], wk2[...])
    kc = _rope(_rms(kc, kn_ref[0:1]), cos_ref[0], sin_ref[0]).astype(BF16)
    k_out[0] = kc
    n2 = jnp.full(kmax_out.shape[1:], _max_sq_norm(kc), F32)

    @pl.when(pl.program_id(1) == 0)
    def _():
        kmax_out[0] = n2

    @pl.when(pl.program_id(1) > 0)
    def _():
        kmax_out[0] = jnp.maximum(kmax_out[0], n2)

    v_out[0] = _compress_mlp(av_ref[0], avn_ref[0], pev_ref[...].astype(BF16), wva[...], wvb[...],
                             bv_ref[...], wv2[...]).astype(BF16)


def _compress(ak, av, pek2, wk, bk, wk2, pev2, wv, bv, wv2, k_norm, cosc, sinc):
    bg, nh, kk = ak.shape
    hid = bk.shape[1]
    tiles = nh // CMP_TILE
    cur = pl.BlockSpec((1, CMP_TILE, kk), lambda g, t: (g, t, 0))
    per = CMP_TILE // 16
    last = nh // 16 - 1
    nxt = pl.BlockSpec((1, 16, kk), lambda g, t: (g, jnp.minimum((t + 1) * per, last), 0))
    full = lambda shp: pl.BlockSpec(shp, lambda g, t: tuple(0 for _ in shp))
    mlp = [full((8, kk)), full((kk, hid)), full((kk, hid)), full((1, hid)), full((hid, HEAD_DIM))]
    out = pl.BlockSpec((1, CMP_TILE, HEAD_DIM), lambda g, t: (g, t, 0))
    tab = pl.BlockSpec((1, CMP_TILE, HEAD_DIM), lambda g, t: (g // N_KV, t, 0))
    oshape = jax.ShapeDtypeStruct((bg, nh, HEAD_DIM), BF16)
    return pl.pallas_call(
        _compress_kernel,
        grid=(bg, tiles),
        in_specs=[cur, nxt, cur, nxt] + mlp + mlp + [full((3, HEAD_DIM)), tab, tab],
        out_specs=[out, out, pl.BlockSpec((1, 8, LANES), lambda g, t: (g, 0, 0))],
        out_shape=[oshape, oshape, jax.ShapeDtypeStruct((bg, 8, LANES), F32)],
        compiler_params=_cparams(("arbitrary", "arbitrary")),
        name="compress",
    )(ak, ak, av, av, pek2, *wk, bk, wk2, pev2, *wv, bv, wv2, k_norm, cosc, sinc)


def _count(mask):
    return jnp.sum(jnp.where(mask, 1.0, 0.0), axis=0, keepdims=True)


def _select_blocks(imp, t_pos):
    n_blocks = imp.shape[0]
    j = lax.broadcasted_iota(jnp.int32, (n_blocks, 1), 0)
    cur = jnp.right_shift(t_pos, SEL_BLOCK.bit_length() - 1)
    valid = j * SEL_BLOCK <= t_pos
    forced = (j == 0) | (j == cur) | (j == cur - 1)
    cand = valid & jnp.logical_not(forced)
    n_forced = 1.0 + jnp.where(cur >= 1, 1.0, 0.0) + jnp.where(cur >= 2, 1.0, 0.0)
    k = float(SEL_TOP_N) - n_forced

    imp_c = jnp.where(cand, imp, -1.0)

    def value_step(it, tb):
        trial = tb | jnp.left_shift(jnp.int32(1), 30 - it)
        keep = _count(imp_c >= pltpu.bitcast(trial, F32)) >= k
        return jnp.where(keep, trial, tb)

    tb = lax.fori_loop(0, 31, value_step, jnp.zeros(t_pos.shape, jnp.int32))
    thr = pltpu.bitcast(tb, F32)
    above = imp_c > thr
    need = k - _count(above)

    j_tied = jnp.where(imp_c == thr, j, 2 * n_blocks)

    def index_step(it, jb):
        trial = jb | jnp.left_shift(jnp.int32(1), (n_blocks.bit_length() - 1) - it)
        keep = _count(j_tied < trial) < need
        return jnp.where(keep, trial, jb)

    def index_search():
        return lax.fori_loop(0, n_blocks.bit_length(), index_step,
                             jnp.zeros(t_pos.shape, jnp.int32))

    all_tied_taken = jnp.min(need - _count(j_tied < 2 * n_blocks)) >= 0.0
    jb = lax.cond(all_tied_taken,
                  lambda: jnp.full(t_pos.shape, 2 * n_blocks - 1, jnp.int32), index_search)
    return jnp.where((forced & valid) | above | (j_tied <= jb), 1.0, 0.0)


def _attn_kernel(q_ref, kc_ref, vc_ref, ksl_ref, vsl_ref, kw_ref, vw_ref, kmax_ref, kcmax_ref,
                 gate_ref, gn_ref, ovt_ref, o_ref, qa_ref, m_ref, l_ref, acc_ref, oslc_ref, s_ref,
                 ocmp_ref, impt_ref, owin_ref):
    gi = pl.program_id(1)
    i = pl.program_id(2)
    tq = q_ref.shape[2]
    rows = GROUP_R * tq
    nc = kc_ref.shape[1]
    nsp = ovt_ref.shape[0]
    ck = SEL_CHUNK
    blocks_per_chunk = ck // SEL_BLOCK

    q_hi = q_ref[0].reshape(rows, HEAD_DIM)
    t_row = i * tq + jnp.bitwise_and(lax.broadcasted_iota(jnp.int32, (rows, 1), 0), tq - 1)

    qf = q_hi.astype(F32)
    k2max = jnp.maximum(kmax_ref[0, 0][0:1, 0:1], kcmax_ref[0][0:1, 0:1])
    bound = jnp.sqrt(jnp.sum(qf * qf, axis=-1, keepdims=True) * k2max) * 1.01
    safe = jnp.max(bound) <= SAFE_BOUND
    unsafe = jnp.logical_not(safe)
    shift = SHIFT_MARGIN - bound

    def importance(pn, ncols):
        psum = pn(0)
        for r in range(1, GROUP_R):
            psum = psum + pn(r)
        impt_ref[...] = _dot_nt(ovt_ref[:, 0:ncols], psum.astype(BF16))

    def visible(ncols):
        n_end = lax.broadcasted_iota(jnp.int32, (1, ncols), 1) * CMP_STRIDE + (CMP_BLOCK - 1)
        return n_end <= t_row

    def compressed_fixed_shift(ncols):
        s = _dot_nt(q_hi, kc_ref[0, 0:ncols, :])
        p = jnp.where(visible(ncols), jnp.exp2(s + shift), 0.0)
        l = jnp.sum(p, axis=-1, keepdims=True)
        some = l > 0.0
        inv = jnp.where(some, 1.0 / jnp.where(some, l, 1.0), 0.0)
        ocmp_ref[...] = _dot(p.astype(BF16), vc_ref[0, 0:ncols, :]) * inv
        importance(lambda r: p[r * tq:(r + 1) * tq] * inv[r * tq:(r + 1) * tq], ncols)

    first_quarter = 4 * (i + 1) <= pl.num_programs(2)
    first_half = 2 * (i + 1) <= pl.num_programs(2)

    @pl.when(safe & first_quarter)
    def _():
        compressed_fixed_shift(nc // 4)

    @pl.when(safe & first_half & jnp.logical_not(first_quarter))
    def _():
        compressed_fixed_shift(nc // 2)

    @pl.when(safe & jnp.logical_not(first_half))
    def _():
        compressed_fixed_shift(nc)

    t_q = i * tq + lax.broadcasted_iota(jnp.int32, (tq, 1), 0)

    def head_rows(r):
        return pl.ds(pl.multiple_of(r * tq, tq), tq)

    @pl.when(unsafe)
    def _():
        impt_ref[...] = jnp.zeros(impt_ref.shape, F32)
        n_end = lax.broadcasted_iota(jnp.int32, (1, nc), 1) * CMP_STRIDE + (CMP_BLOCK - 1)

        def head(r, carry):
            s = jnp.where(n_end <= t_q, _dot_nt(q_ref[0, r], kc_ref[0]), NEG_INF)
            m = jnp.max(s, axis=-1, keepdims=True)
            p = jnp.exp2(s - m)
            l = jnp.sum(p, axis=-1, keepdims=True)
            pc = (p * jnp.where(m > 0.5 * NEG_INF, 1.0 / l, 0.0)).astype(BF16)
            ocmp_ref[head_rows(r), :] = _dot(pc, vc_ref[0])
            impt_ref[...] += _dot_nt(ovt_ref[...], pc)
            return carry

        lax.fori_loop(0, GROUP_R, head, 0)

    o_cmp = ocmp_ref[...]

    t_pos = i * tq + lax.broadcasted_iota(jnp.int32, (1, tq), 1)
    @pl.when(first_half)
    def _():
        impt_ref[0:nsp // 2, :] = _select_blocks(impt_ref[0:nsp // 2, :], t_pos)
        impt_ref[nsp // 2:, :] = jnp.zeros((nsp // 2, tq), F32)

    @pl.when(jnp.logical_not(first_half))
    def _():
        impt_ref[...] = _select_blocks(impt_ref[...], t_pos)

    sel = jnp.transpose(impt_ref[...])
    sel = jnp.concatenate([sel] * GROUP_R, axis=0) > 0.5

    def augmented_keys(c):
        k0 = pl.multiple_of(c * ck, ck)
        kblk = c * blocks_per_chunk + jnp.right_shift(
            lax.broadcasted_iota(jnp.int32, (ck, 1), 0), SEL_BLOCK.bit_length() - 1)
        lane = lax.broadcasted_iota(jnp.int32, (1, LANES), 1)
        onehot = jnp.where(jnp.bitwise_and(kblk, LANES - 1) == lane, 1.0, 0.0).astype(BF16)
        return k0, jnp.concatenate([ksl_ref[0, 0, pl.ds(k0, ck), :], onehot], axis=1)

    def scores(c):
        _, ka = augmented_keys(c)
        sc = _dot_nt(qa_ref[c // (LANES // blocks_per_chunk)], ka)
        kp = c * ck + lax.broadcasted_iota(jnp.int32, (1, ck), 1)
        return jnp.where(kp <= t_row, sc, NEG_INF)

    def set_lane_values(selected_value):
        lanev = jnp.where(sel, selected_value, NEG_INF).astype(BF16)
        for w in range(nsp // LANES):
            qa_ref[w] = jnp.concatenate([q_hi, lanev[:, w * LANES:(w + 1) * LANES]], axis=1)

    c_last = (i * tq) // ck

    wk = WINDOW + tq
    start = pl.multiple_of(jnp.maximum(i * tq - WINDOW, 0), tq)

    def window_fixed_shift():
        sw = _dot_nt(q_hi, kw_ref[0, 0, pl.ds(start, wk), :]) + shift
        parts = []
        for b in range(wk // tq):
            kp = start + b * tq + lax.broadcasted_iota(jnp.int32, (1, tq), 1)
            ok = kp <= t_row
            if b == 0:
                ok = ok & (kp > t_row - WINDOW)
            parts.append(jnp.where(ok, jnp.exp2(sw[:, b * tq:(b + 1) * tq]), 0.0).astype(BF16))
        va = jnp.concatenate([vw_ref[0, 0, pl.ds(start, wk), :], jnp.ones((wk, HEAD_DIM), BF16)], axis=1)
        acc = _dot(jnp.concatenate(parts, axis=1), va)
        owin_ref[...] = acc[:, 0:HEAD_DIM] / acc[:, HEAD_DIM:HEAD_DIM + 1]

    @pl.when(safe)
    def _():
        window_fixed_shift()
        set_lane_values(shift)
        acc_ref[...] = jnp.zeros(acc_ref.shape, F32)
        ones = jnp.ones((ck, HEAD_DIM), BF16)

        def produce(c, slot):
            s_ref[slot] = scores(jnp.minimum(c, c_last))

        def consume(c, slot):
            k0 = pl.multiple_of(c * ck, ck)
            va = jnp.concatenate([vsl_ref[0, 0, pl.ds(k0, ck), :], ones], axis=1)
            acc_ref[...] += _dot(jnp.exp2(s_ref[slot]).astype(BF16), va)

        def pair(kk, carry):
            c = 2 * kk
            produce(c + 1, 1)
            consume(c, 0)
            produce(c + 2, 0)
            consume(c + 1, 1)
            return carry

        n_chunks = c_last + 1
        produce(0, 0)
        lax.fori_loop(0, n_chunks // 2, pair, 0)

        @pl.when(n_chunks % 2 == 1)
        def _():
            consume(c_last, 0)

        oslc_ref[...] = acc_ref[:, 0:HEAD_DIM] / acc_ref[:, HEAD_DIM:HEAD_DIM + 1]

    @pl.when(unsafe)
    def _():
        set_lane_values(0.0)
        kp_w = start + lax.broadcasted_iota(jnp.int32, (1, wk), 1)

        def head(r, carry):
            rs = head_rows(r)
            sw = _dot_nt(q_ref[0, r], kw_ref[0, 0, pl.ds(start, wk), :])
            sw = jnp.where((kp_w <= t_q) & (kp_w > t_q - WINDOW), sw, NEG_INF)
            pw = jnp.exp2(sw - jnp.max(sw, axis=-1, keepdims=True))
            owin_ref[rs, :] = (_dot(pw.astype(BF16), vw_ref[0, 0, pl.ds(start, wk), :])
                               / jnp.sum(pw, axis=-1, keepdims=True))

            m_ref[rs, :] = jnp.full((tq, 1), NEG_INF, F32)
            l_ref[rs, :] = jnp.zeros((tq, 1), F32)
            acc_ref[rs, 0:HEAD_DIM] = jnp.zeros((tq, HEAD_DIM), F32)

            def chunk(c, carry2):
                k0, ka = augmented_keys(c)
                sc = _dot_nt(qa_ref[c // (LANES // blocks_per_chunk), rs, :], ka)
                kp = c * ck + lax.broadcasted_iota(jnp.int32, (1, ck), 1)
                sc = jnp.where(kp <= t_q, sc, NEG_INF)
                m_old = m_ref[rs, :]
                m_new = jnp.maximum(m_old, jnp.max(sc, axis=-1, keepdims=True))
                alpha = jnp.exp2(m_old - m_new)
                pe = jnp.exp2(sc - m_new)
                l_ref[rs, :] = alpha * l_ref[rs, :] + jnp.sum(pe, axis=-1, keepdims=True)
                acc_ref[rs, 0:HEAD_DIM] = alpha * acc_ref[rs, 0:HEAD_DIM] + _dot(
                    pe.astype(BF16), vsl_ref[0, 0, pl.ds(k0, ck), :])
                m_ref[rs, :] = m_new
                return carry2

            lax.fori_loop(0, c_last + 1, chunk, 0)
            oslc_ref[rs, :] = acc_ref[rs, 0:HEAD_DIM] / l_ref[rs, :]
            return carry

        lax.fori_loop(0, GROUP_R, head, 0)

    o_slc = oslc_ref[...]
    o_win = owin_ref[...]

    gates = gate_ref[0]
    gn = gn_ref[...]
    n_gate = 3 * GROUP_R
    for r in range(GROUP_R):
        sl = slice(r * tq, (r + 1) * tq)

        def gate(branch, r=r):
            col = 3 * r + branch
            return jnp.where(gi == 0, gates[:, col:col + 1], gates[:, n_gate + col:n_gate + col + 1])

        o = gate(0) * o_cmp[sl] + gate(1) * o_slc[sl] + gate(2) * o_win[sl]
        hs = slice(r * HEAD_DIM, (r + 1) * HEAD_DIM)
        o_ref[0, :, hs] = _rms(o, gn[:, hs]).astype(o_ref.dtype)


def _attn(q_r, k_cmp, v_cmp, ksl, vsl, kw, vw, kmax, kcmax, gates, gain, ovt):
    batch, _, seq, _ = q_r.shape
    nc = k_cmp.shape[1]
    nsp = ovt.shape[0]
    tq = Q_TILE
    rows = GROUP_R * tq
    gw = GROUP_R * HEAD_DIM
    cmp_spec = pl.BlockSpec((1, nc, HEAD_DIM), lambda b, g, i: (b * N_KV + g, 0, 0))
    kv_spec = pl.BlockSpec((1, 1, seq, HEAD_DIM), lambda b, g, i: (b, g, 0, 0),
                           pipeline_mode=pl.Buffered(1))
    return pl.pallas_call(
        _attn_kernel,
        grid=(batch, N_KV, seq // tq),
        in_specs=[pl.BlockSpec((1, GROUP_R, tq, HEAD_DIM), lambda b, g, i: (b, g, i, 0)),
                  cmp_spec, cmp_spec,
                  kv_spec, kv_spec, kv_spec, kv_spec,
                  pl.BlockSpec((1, 1, 8, LANES), lambda b, g, i: (b, g, 0, 0)),
                  pl.BlockSpec((1, 8, LANES), lambda b, g, i: (b * N_KV + g, 0, 0)),
                  pl.BlockSpec((1, tq, LANES), lambda b, g, i: (b, i, 0)),
                  pl.BlockSpec((1, gw), lambda b, g, i: (0, g)),
                  pl.BlockSpec((nsp, nc), lambda b, g, i: (0, 0))],
        out_specs=pl.BlockSpec((1, tq, gw), lambda b, g, i: (b, i, g)),
        out_shape=jax.ShapeDtypeStruct((batch, seq, N_KV * gw), BF16),
        scratch_shapes=[pltpu.VMEM((nsp // LANES, rows, 2 * HEAD_DIM), BF16),
                        pltpu.VMEM((rows, 1), F32), pltpu.VMEM((rows, 1), F32),
                        pltpu.VMEM((rows, 2 * HEAD_DIM), F32),
                        pltpu.VMEM((rows, HEAD_DIM), F32),
                        pltpu.VMEM((2, rows, SEL_CHUNK), F32),
                        pltpu.VMEM((rows, HEAD_DIM), F32),
                        pltpu.VMEM((nsp, tq), F32),
                        pltpu.VMEM((rows, HEAD_DIM), F32)],
        compiler_params=_cparams(("arbitrary", "arbitrary", "arbitrary")),
        name="attn",
    )(q_r, k_cmp, v_cmp, ksl, vsl, kw, vw, kmax, kcmax, gates, gain, ovt)


def _outproj_kernel(yc_ref, ya_ref, x_ref, w_ref, ga_ref, g_ref, sh_ref, sc_ref, x1_ref, h2_ref):
    cw = yc_ref.shape[1]
    acc = _dot(yc_ref[...], w_ref[0:cw, :]) + _dot(ya_ref[...], w_ref[cw:, :])
    x1 = x_ref[...] + ga_ref[0] * acc
    x1_ref[...] = x1
    h2_ref[...] = (_rms(x1, g_ref[...]) * (1.0 + sc_ref[0]) + sh_ref[0]).astype(h2_ref.dtype)


def _outproj(y_conv, y_attn, x2, w_out, mod3, gain, seq):
    t, d = x2.shape
    tiles_per_seq = seq // ROW_TILE
    mod = lambda k: pl.BlockSpec((1, 1, d), lambda i: (i // tiles_per_seq, 0, k))
    row = lambda w: pl.BlockSpec((ROW_TILE, w), lambda i: (i, 0))
    return pl.pallas_call(
        _outproj_kernel,
        grid=(t // ROW_TILE,),
        in_specs=[row(y_conv.shape[1]), row(y_attn.shape[1]), row(d),
                  pl.BlockSpec(w_out.shape, lambda i: (0, 0)),
                  mod(2), pl.BlockSpec((1, d), lambda i: (0, 0)), mod(3), mod(4)],
        out_specs=[row(d), row(d)],
        out_shape=[jax.ShapeDtypeStruct((t, d), F32), jax.ShapeDtypeStruct((t, d), BF16)],
        compiler_params=_cparams(("arbitrary",)),
        name="outproj",
    )(y_conv, y_attn, x2, w_out, mod3, gain, mod3, mod3)


def _ffn_kernel(h_ref, w1_ref, w3_ref, w2_ref, x1_ref, gf_ref, o_ref):
    f = pl.program_id(1)
    last = pl.num_programs(1) - 1
    sub = h_ref.shape[0] // FFN_SUB

    def up(r):
        h = h_ref[r * sub:(r + 1) * sub, :]
        return _dot(h, w1_ref[...]), _dot(h, w3_ref[...])

    def down(r, ab):
        rs = slice(r * sub, (r + 1) * sub)
        o_ref[rs, :] += _dot((jax.nn.silu(ab[0]) * ab[1]).astype(BF16), w2_ref[...])

    @pl.when(f == 0)
    def _():
        o_ref[...] = jnp.zeros(o_ref.shape, F32)

    ab = up(0)
    for r in range(FFN_SUB):
        ab_next = up(r + 1) if r + 1 < FFN_SUB else None
        down(r, ab)
        ab = ab_next

    @pl.when(f == last)
    def _():
        o_ref[...] = x1_ref[...] + gf_ref[0] * o_ref[...]


def _ffn(h2, w1, w3, w2, x1, mod3, seq):
    t, d = x1.shape
    dff = w1.shape[1]
    tm = FFN_TM
    tiles_per_seq = seq // tm
    return pl.pallas_call(
        _ffn_kernel,
        grid=(t // tm, dff // FFN_TF),
        in_specs=[pl.BlockSpec((tm, d), lambda i, f: (i, 0)),
                  pl.BlockSpec((d, FFN_TF), lambda i, f: (0, f)),
                  pl.BlockSpec((d, FFN_TF), lambda i, f: (0, f)),
                  pl.BlockSpec((FFN_TF, d), lambda i, f: (f, 0)),
                  pl.BlockSpec((tm, d), lambda i, f: (i, 0)),
                  pl.BlockSpec((1, 1, d), lambda i, f: (i // tiles_per_seq, 0, 5))],
        out_specs=pl.BlockSpec((tm, d), lambda i, f: (i, 0)),
        out_shape=jax.ShapeDtypeStruct((t, d), F32),
        compiler_params=_cparams(("arbitrary", "arbitrary"), FFN_VMEM_LIMIT),
        name="ffn",
    )(h2, w1, w3, w2, x1, mod3)


def _overlap_matrix(n_cmp_pad, n_sel, n_sel_pad):
    cmp_start = np.arange(n_cmp_pad) * CMP_STRIDE
    cmp_end = cmp_start + CMP_BLOCK - 1
    sel_start = np.arange(n_sel) * SEL_BLOCK
    ov = ((cmp_start[None, :] < sel_start[:, None] + SEL_BLOCK)
          & (cmp_end[None, :] >= sel_start[:, None])).astype(np.float32)
    return np.pad(ov, ((0, n_sel_pad - n_sel), (0, 0)))


def _w1_halves(w1):
    l, dk, hid = w1.shape
    half = l // 2
    return w1[:half].reshape(half * dk, hid), w1[half:].reshape(half * dk, hid)


def _pe_rows(pe):
    l, dk = pe.shape
    half = l // 2
    rows = jnp.stack([pe[:half].reshape(half * dk), pe[half:].reshape(half * dk)])
    return jnp.pad(rows, ((0, 6), (0, 0)))


def kernel(x, c, positions, ada_w, ada_b, norm_mix, norm_ffn, w_in, conv_w, cmp_pe_k, cmp_k_w1, cmp_k_b1, cmp_k_w2, cmp_pe_v, cmp_v_w1, cmp_v_b1, cmp_v_w2, q_norm, k_norm, out_norm_conv, out_norm_attn, w_out, ffn_w1, ffn_w3, ffn_w2):
    batch, seq, d = x.shape
    t = batch * seq
    depth = ada_w.shape[0]
    cw = conv_w.shape[2]
    aw = N_HEADS * HEAD_DIM
    kvw = N_KV * HEAD_DIM
    n_gate = 3 * N_HEADS
    assert seq % max(ROW_TILE, SEL_CHUNK, FFN_TM) == 0 and seq >= WINDOW + Q_TILE
    assert seq // SEL_BLOCK >= SEL_TOP_N and CMP_BLOCK == 2 * CMP_STRIDE
    nh = seq // CMP_STRIDE
    n_sel = seq // SEL_BLOCK
    n_sel_pad = -(-n_sel // LANES) * LANES
    n_sel_pad = 1 << (n_sel_pad - 1).bit_length()
    assert nh % CMP_TILE == 0

    inv = 1.0 / (ROPE_THETA ** (jnp.arange(0, HEAD_DIM, 2, dtype=F32) / HEAD_DIM))
    inv_full = jnp.concatenate([inv, inv])[None, :]
    sign = jnp.concatenate([-jnp.ones(HEAD_DIM // 2, F32), jnp.ones(HEAD_DIM // 2, F32)])[None, :]
    cosf, sinf = _rope_tables(positions.reshape(t, 1), inv_full, sign)
    at_end = lambda tab: jnp.pad(
        tab.reshape(batch, nh, CMP_STRIDE, HEAD_DIM)[:, 1:, CMP_STRIDE - 1], ((0, 0), (0, 1), (0, 0)))
    cosc, sinc = at_end(cosf), at_end(sinf)
    ovt = jnp.asarray(_overlap_matrix(nh, n_sel, n_sel_pad), BF16)

    c8 = jnp.pad(c, ((0, 8 - batch), (0, 0)))
    x2 = x.reshape(t, d)
    for l in range(depth):
        mod = _ada(c8, ada_w[l], ada_b[l][None, :])
        mod3 = mod[:batch].reshape(batch, 1, 6 * d)

        w = w_in[l]
        n_in = w.shape[1]
        o_q = 3 * cw
        o_gl = o_q + aw + 6 * kvw
        assert o_gl + n_gate == n_in
        n_pad = -(-n_in // MXU_TILE) * MXU_TILE
        w_pad = jnp.pad(w, ((0, 0), (0, n_pad - n_in))).astype(BF16)
        pm = _inproj(x2, mod3, norm_mix[l][None, :], w_pad, seq)

        y_conv = _conv(pm, conv_w[l], out_norm_conv[l][None, :], seq, cw)

        q_r, kc_t, vc_t, ksl, vsl, kw, vw, gates, kmax = _prep(
            pm, cosf, sinf, q_norm[l][None, :], k_norm[l], batch, seq, o_q, o_gl)

        as_bf16 = lambda ws: tuple(wi.astype(BF16) for wi in ws)
        k_cmp, v_cmp, kcmax = _compress(
            kc_t.reshape(batch * N_KV, nh, CMP_STRIDE * HEAD_DIM),
            vc_t.reshape(batch * N_KV, nh, CMP_STRIDE * HEAD_DIM),
            _pe_rows(cmp_pe_k[l]), as_bf16(_w1_halves(cmp_k_w1[l])), cmp_k_b1[l][None, :],
            cmp_k_w2[l].astype(BF16),
            _pe_rows(cmp_pe_v[l]), as_bf16(_w1_halves(cmp_v_w1[l])), cmp_v_b1[l][None, :],
            cmp_v_w2[l].astype(BF16), k_norm[l], cosc, sinc)

        y_attn = _attn(q_r, k_cmp, v_cmp, ksl, vsl, kw, vw, kmax, kcmax, gates,
                       out_norm_attn[l][None, :], ovt)

        x1, h2 = _outproj(y_conv, y_attn.reshape(t, aw), x2, w_out[l].astype(BF16), mod3,
                          norm_ffn[l][None, :], seq)
        x2 = _ffn(h2, ffn_w1[l].astype(BF16), ffn_w3[l].astype(BF16), ffn_w2[l].astype(BF16),
                  x1, mod3, seq)
    return x2.reshape(batch, seq, d)
```
